```python
import math
import jax, jax.numpy as jnp
from jax import lax
import numpy as np

D_MODEL = 2048
BATCH = 4
SEQ = 2048
DEPTH = 4
DEC_BATCH = 128
DEC_SEQ = 1
PAST_LEN = 16384
PAGE_SIZE = 128

N_EVEN = (DEPTH + 1) // 2
N_ODD = DEPTH // 2
CHUNK = 64
CONV_W = 4
NORM_EPS = 1e-5
RET_HEADS = 8
RET_DK = D_MODEL // RET_HEADS
RET_DV = D_MODEL // RET_HEADS
ROPE_BASE = 10000.0
SSM_HEADS = 32
SSM_P = D_MODEL // SSM_HEADS
SSM_INNER = SSM_HEADS * SSM_P
SSM_GROUPS = 4
SSM_N = 128
SSM_CONV_CH = SSM_INNER + 2 * SSM_GROUPS * SSM_N
GDN_HK = 16
GDN_HV = 32
GDN_DK = 128
GDN_DV = 128
GDN_QK = GDN_HK * GDN_DK
GDN_VW = GDN_HV * GDN_DV
GDN_CONV_CH = 2 * GDN_QK + GDN_VW
EVEN_SPLITS = (RET_HEADS * RET_DK, RET_HEADS * RET_DK, RET_HEADS * RET_DV, RET_HEADS * RET_DV, SSM_INNER, SSM_CONV_CH, SSM_HEADS)
EVEN_IN = sum(EVEN_SPLITS)
EVEN_MIX = RET_HEADS * RET_DV + SSM_INNER
ODD_SPLITS = (GDN_CONV_CH, GDN_VW, GDN_HV, GDN_HV)
ODD_IN = sum(ODD_SPLITS)
N_EXPERTS = 64
TOP_K = 8
D_EXPERT = D_MODEL // 4
D_SHARED = D_MODEL // 4
ROUTED_SCALE = 2.5
EXPERT_BLOCK = 128
D_PLE = 256
ALPHA = (2 * DEPTH) ** 0.25
BETA = (8 * DEPTH) ** -0.25

kernel_name = 'hybrid_retention_ssd_gdn_moe_step'


def _split(t, sizes):
    out, start = [], 0
    for s in sizes:
        out.append(t[..., start:start + s])
        start += s
    return out


def _layer_norm(x, g, b):
    xf = x.astype(jnp.float32)
    xc = xf - jnp.mean(xf, -1, keepdims=True)
    var = jnp.mean(xc * xc, -1, keepdims=True)
    y = xc * lax.rsqrt(var + NORM_EPS) * g.astype(jnp.float32) + b.astype(jnp.float32)
    return y.astype(x.dtype)


def _rms(xf):
    return xf * lax.rsqrt(jnp.mean(xf * xf, -1, keepdims=True) + NORM_EPS)


def _head_ln(xf):
    xc = xf - jnp.mean(xf, -1, keepdims=True)
    return xc * lax.rsqrt(jnp.mean(xc * xc, -1, keepdims=True) + NORM_EPS)


def _l2norm(xf):
    return xf * lax.rsqrt(jnp.sum(xf * xf, -1, keepdims=True) + 1e-6)


def _rotary(t, pos):
    half = t.shape[-1] // 2
    inv = ROPE_BASE ** (-jnp.arange(half, dtype=jnp.float32) / half)
    ang = pos.astype(jnp.float32)[:, None] * inv
    cos = jnp.cos(ang)[None, :, None, :]
    sin = jnp.sin(ang)[None, :, None, :]
    t1, t2 = t[..., :half], t[..., half:]
    return jnp.concatenate([t1 * cos - t2 * sin, t1 * sin + t2 * cos], -1)


def _chunk_len(L):
    return CHUNK if L % CHUNK == 0 else L


def _to_chunks(t, c):
    return t.reshape(t.shape[0], t.shape[1] // c, c, *t.shape[2:]).swapaxes(0, 1)


def _from_chunks(t):
    t = t.swapaxes(0, 1)
    return t.reshape(t.shape[0], t.shape[1] * t.shape[2], *t.shape[3:])


def _causal_conv(x, prev, w, b):
    L = x.shape[1]
    xp = jnp.concatenate([prev.astype(x.dtype), x], axis=1)
    y = xp[:, CONV_W - 1:] * w[CONV_W - 1]
    for i in range(CONV_W - 1):
        y = y + xp[:, i:i + L] * w[i]
    if b is not None:
        y = y + b
    return y, xp[:, L:]


def _retention_scan(q, k, v, s0, log_gamma):
    L = q.shape[1]
    c = _chunk_len(L)
    idx = jnp.arange(c, dtype=jnp.float32)
    diff = idx[:, None] - idx[None, :]
    dmask = jnp.exp(jnp.where((diff >= 0)[None], log_gamma[:, None, None] * diff[None], -jnp.inf))
    q_dec = jnp.exp((idx[:, None] + 1.0) * log_gamma)[:, :, None]
    k_dec = jnp.exp((c - 1.0 - idx)[:, None] * log_gamma)[:, :, None]
    c_dec = jnp.exp(c * log_gamma)[:, None, None]

    def step(s, inp):
        qc, kc, vc = inp
        att = jnp.einsum('bihd,bjhd->bhij', qc, kc) * dmask
        o = jnp.einsum('bhij,bjhe->bihe', att, vc) + jnp.einsum('bihd,bhde->bihe', qc * q_dec, s)
        s = s * c_dec + jnp.einsum('bjhd,bjhe->bhde', kc * k_dec, vc)
        return s, o

    s, o = lax.scan(step, s0, (_to_chunks(q, c), _to_chunks(k, c), _to_chunks(v, c)))
    return _from_chunks(o), s


def _ssd_scan(x, dt, a, bm, cm, s0):
    L = x.shape[1]
    c = _chunk_len(L)
    ar = jnp.arange(c)
    mask5 = (ar[:, None] >= ar[None, :])[None, :, :, None, None]

    def step(s, inp):
        xc, dtc, bc, cc = inp
        cs = jnp.cumsum(dtc * a, axis=1)
        lmat = jnp.exp(jnp.where(mask5, cs[:, :, None] - cs[:, None, :], -jnp.inf))
        cb = jnp.einsum('bign,bjgn->bijg', cc, bc)
        y = jnp.einsum('bijg,bijgr,bjgr,bjgrp->bigrp', cb, lmat, dtc, xc)
        y = y + jnp.einsum('bign,bgrpn->bigrp', cc, s) * jnp.exp(cs)[..., None]
        last = cs[:, -1]
        w_end = jnp.exp(last[:, None] - cs) * dtc
        s = s * jnp.exp(last)[..., None, None] + jnp.einsum('bjgr,bjgn,bjgrp->bgrpn', w_end, bc, xc)
        return s, y

    s, y = lax.scan(step, s0, (_to_chunks(x, c), _to_chunks(dt, c), _to_chunks(bm, c), _to_chunks(cm, c)))
    return _from_chunks(y), s


def _gdn_scan(q, k, v, g, beta, s0):
    L = q.shape[1]
    dv = v.shape[-1]
    c = _chunk_len(L)
    ar = jnp.arange(c)
    incl = ar[:, None] >= ar[None, :]
    strict = ar[:, None] > ar[None, :]
    eye = jnp.eye(c, dtype=jnp.float32)

    def step(s, inp):
        qc, kc, vc, gc, bc = inp
        gcum = jnp.cumsum(gc, axis=1)
        gh = gcum.swapaxes(1, 2)
        decay = jnp.exp(jnp.where(incl, gh[..., :, None] - gh[..., None, :], -jnp.inf))
        kb = kc * bc[..., None]
        a_mat = jnp.where(strict, jnp.einsum('bihd,bjhd->bhij', kb, kc) * decay, 0.0)
        rhs = jnp.concatenate([(vc * bc[..., None]).swapaxes(1, 2),
                               (kb * jnp.exp(gcum)[..., None]).swapaxes(1, 2)], -1)
        sol = lax.linalg.triangular_solve(a_mat + eye, rhs, left_side=True, lower=True, unit_diagonal=True)
        u = sol[..., :dv] - jnp.einsum('bhid,bhde->bhie', sol[..., dv:], s)
        qk = jnp.einsum('bihd,bjhd->bhij', qc, kc) * decay
        o = jnp.einsum('bihd,bhde->bihe', qc * jnp.exp(gcum)[..., None], s) + jnp.einsum('bhij,bhje->bihe', qk, u)
        g_last = gcum[:, -1]
        s = s * jnp.exp(g_last)[..., None, None] + jnp.einsum(
            'bjhd,bhje->bhde', kc * jnp.exp(g_last[:, None] - gcum)[..., None], u)
        return s, o

    xs = (_to_chunks(q, c), _to_chunks(k, c), _to_chunks(v, c), _to_chunks(g, c), _to_chunks(beta, c))
    s, o = lax.scan(step, s0, xs)
    return _from_chunks(o), s


def _even_mixer(x, pos, s_ret, s_ssm, conv_prev, w_in, w_out, conv_w, conv_b, dt_bias, a_log, d_skip, norm_w):
    f32 = jnp.float32
    bsz, L, _ = x.shape
    q, k, v, g, z, xbc, dt = _split(x @ w_in, EVEN_SPLITS)
    q = _rotary(q.astype(f32).reshape(bsz, L, RET_HEADS, RET_DK), pos)
    k = _rotary(k.astype(f32).reshape(bsz, L, RET_HEADS, RET_DK), pos) * RET_DK ** -0.5
    v = v.astype(f32).reshape(bsz, L, RET_HEADS, RET_DV)
    log_gamma = jnp.log1p(-jnp.exp2(-5.0 - jnp.arange(RET_HEADS, dtype=f32)))
    o_ret, s_ret_new = _retention_scan(q, k, v, s_ret.astype(f32), log_gamma)
    o_ret = jax.nn.silu(g.astype(f32)) * _head_ln(o_ret).reshape(bsz, L, -1)
    xbc, conv_new = _causal_conv(xbc, conv_prev, conv_w, conv_b)
    xs, bm, cm = _split(jax.nn.silu(xbc.astype(f32)), (SSM_INNER, SSM_GROUPS * SSM_N, SSM_GROUPS * SSM_N))
    r = SSM_HEADS // SSM_GROUPS
    xs = xs.reshape(bsz, L, SSM_GROUPS, r, SSM_P)
    dt = jax.nn.softplus(dt.astype(f32) + dt_bias.astype(f32)).reshape(bsz, L, SSM_GROUPS, r)
    a = -jnp.exp(a_log.astype(f32)).reshape(SSM_GROUPS, r)
    y, s_ssm_new = _ssd_scan(xs, dt, a, bm.reshape(bsz, L, SSM_GROUPS, SSM_N), cm.reshape(bsz, L, SSM_GROUPS, SSM_N),
                             s_ssm.astype(f32).reshape(bsz, SSM_GROUPS, r, SSM_P, SSM_N))
    y = y + d_skip.astype(f32).reshape(SSM_GROUPS, r, 1) * xs
    y = y.reshape(bsz, L, SSM_INNER) * jax.nn.silu(z.astype(f32))
    y = _rms(y.reshape(bsz, L, SSM_GROUPS, -1)).reshape(bsz, L, SSM_INNER) * norm_w.astype(f32)
    out = jnp.concatenate([o_ret, y], -1).astype(x.dtype) @ w_out
    return (out, s_ret_new.astype(s_ret.dtype),
            s_ssm_new.reshape(bsz, SSM_HEADS, SSM_P, SSM_N).astype(s_ssm.dtype), conv_new.astype(conv_prev.dtype))


def _odd_mixer(x, s_gdn, conv_prev, w_in, w_out, conv_w, dt_bias, a_log, norm_w):
    f32 = jnp.float32
    bsz, L, _ = x.shape
    qkv, z, a, b = _split(x @ w_in, ODD_SPLITS)
    qkv, conv_new = _causal_conv(qkv, conv_prev, conv_w, None)
    q, k, v = _split(jax.nn.silu(qkv.astype(f32)), (GDN_QK, GDN_QK, GDN_VW))
    rep = GDN_HV // GDN_HK
    q = jnp.repeat(_l2norm(q.reshape(bsz, L, GDN_HK, GDN_DK)), rep, axis=2) * GDN_DK ** -0.5
    k = jnp.repeat(_l2norm(k.reshape(bsz, L, GDN_HK, GDN_DK)), rep, axis=2)
    v = v.reshape(bsz, L, GDN_HV, GDN_DV)
    g = -jnp.exp(a_log.astype(f32)) * jax.nn.softplus(a.astype(f32) + dt_bias.astype(f32))
    beta = jax.nn.sigmoid(b.astype(f32))
    o, s_new = _gdn_scan(q, k, v, g, beta, s_gdn.astype(f32))
    o = _rms(o) * norm_w.astype(f32) * jax.nn.silu(z.astype(f32).reshape(bsz, L, GDN_HV, GDN_DV))
    out = o.reshape(bsz, L, GDN_VW).astype(x.dtype) @ w_out
    return out, s_new.astype(s_gdn.dtype), conv_new.astype(conv_prev.dtype)


def _routed_experts(x2, idx, wts, we_gate, we_up, we_down):
    T, D = x2.shape
    n_pairs = T * TOP_K
    n_blocks = (n_pairs + N_EXPERTS * (EXPERT_BLOCK - 1) + EXPERT_BLOCK - 1) // EXPERT_BLOCK
    flat_e = idx.reshape(-1).astype(jnp.int32)
    order = jnp.argsort(flat_e)
    sorted_e = flat_e[order]
    counts = jnp.bincount(flat_e, length=N_EXPERTS).astype(jnp.int32)
    padded = (counts + EXPERT_BLOCK - 1) // EXPERT_BLOCK * EXPERT_BLOCK
    pad_end = jnp.cumsum(padded)
    pad_start = pad_end - padded
    grp_start = jnp.cumsum(counts) - counts
    dest = pad_start[sorted_e] + jnp.arange(n_pairs, dtype=jnp.int32) - grp_start[sorted_e]
    rows_tok = jnp.full((n_blocks * EXPERT_BLOCK,), T, jnp.int32).at[dest].set((order // TOP_K).astype(jnp.int32))
    rows_gate = jnp.zeros((n_blocks * EXPERT_BLOCK,), x2.dtype).at[dest].set(wts.reshape(-1)[order])
    block_expert = jnp.minimum(
        jnp.searchsorted(pad_end, jnp.arange(n_blocks, dtype=jnp.int32) * EXPERT_BLOCK, side='right'), N_EXPERTS - 1)
    x_pad = jnp.concatenate([x2, jnp.zeros((1, D), x2.dtype)], 0)

    def body(acc, blk):
        tok, gate, e = blk
        xb = x_pad[tok]
        h = jax.nn.silu(xb @ we_gate[e]) * (xb @ we_up[e])
        return acc.at[tok].add((h @ we_down[e]) * gate[:, None]), None

    acc, _ = lax.scan(body, jnp.zeros_like(x_pad),
                      (rows_tok.reshape(n_blocks, EXPERT_BLOCK), rows_gate.reshape(n_blocks, EXPERT_BLOCK), block_expert))
    return acc[:T]


def _moe(x2, router_w, router_b, we_gate, we_up, we_down, ws_gate, ws_up, ws_down):
    scores = jax.nn.sigmoid((x2 @ router_w).astype(jnp.float32))
    _, idx = lax.top_k(scores + router_b.astype(jnp.float32), TOP_K)
    wts = jnp.take_along_axis(scores, idx, axis=-1)
    wts = wts / jnp.sum(wts, -1, keepdims=True) * ROUTED_SCALE
    shared = (jax.nn.silu(x2 @ ws_gate) * (x2 @ ws_up)) @ ws_down
    return shared + _routed_experts(x2, idx, wts.astype(x2.dtype), we_gate, we_up, we_down)


def _post_block(x, mix_out, p, ln1_g, ln1_b, ln2_g, ln2_b, router_w, router_b, we_gate, we_up, we_down,
                ws_gate, ws_up, ws_down, ple_proj, ple_gate):
    bsz, L, D = x.shape
    x = _layer_norm(ALPHA * x + mix_out, ln1_g, ln1_b)
    ffn = _moe(x.reshape(-1, D), router_w, router_b, we_gate, we_up, we_down, ws_gate, ws_up, ws_down)
    x = _layer_norm(ALPHA * x + ffn.reshape(bsz, L, D), ln2_g, ln2_b)
    return x + (jax.nn.sigmoid(x @ ple_gate) * (p.astype(x.dtype) @ ple_proj)).astype(x.dtype)


def setup_inputs(seed: int = 0) -> dict:
    key = jax.random.key(seed)
    ks = iter(jax.random.split(key, 64))
    D = D_MODEL

    def nrm(shape, scale):
        return jax.random.normal(next(ks), shape, jnp.float32) * scale

    def gain(shape):
        return 1.0 + nrm(shape, 0.02)

    def dt_bias(shape):
        dt = jnp.exp(jax.random.uniform(next(ks), shape, jnp.float32, math.log(1e-3), math.log(1e-1)))
        return dt + jnp.log(-jnp.expm1(-dt))

    def a_log(shape):
        return jnp.log(jax.random.uniform(next(ks), shape, jnp.float32, 1.0, 16.0))

    return {
        'x_prompt': nrm((BATCH, SEQ, D), 1.0),
        'x_sample': nrm((DEC_BATCH, DEC_SEQ, D), 1.0),
        'state_ret': nrm((N_EVEN, DEC_BATCH, RET_HEADS, RET_DK, RET_DV), 0.1),
        'state_ssm': nrm((N_EVEN, DEC_BATCH, SSM_HEADS, SSM_P, SSM_N), 0.1),
        'state_ssm_conv': nrm((N_EVEN, DEC_BATCH, CONV_W - 1, SSM_CONV_CH), 1.0),
        'state_gdn': nrm((N_ODD, DEC_BATCH, GDN_HV, GDN_DK, GDN_DV), 0.1),
        'state_gdn_conv': nrm((N_ODD, DEC_BATCH, CONV_W - 1, GDN_CONV_CH), 1.0),
        'p_prompt': nrm((DEPTH, BATCH, SEQ, D_PLE), 1.0),
        'p_sample': nrm((DEPTH, DEC_BATCH, DEC_SEQ, D_PLE), 1.0),
        'ev_w_in': nrm((N_EVEN, D, EVEN_IN), D ** -0.5),
        'ev_w_out': nrm((N_EVEN, EVEN_MIX, D), EVEN_MIX ** -0.5 * BETA),
        'ssm_conv_w': nrm((N_EVEN, CONV_W, SSM_CONV_CH), CONV_W ** -0.5),
        'ssm_conv_b': nrm((N_EVEN, SSM_CONV_CH), 0.02),
        'ssm_dt_bias': dt_bias((N_EVEN, SSM_HEADS)),
        'ssm_a_log': a_log((N_EVEN, SSM_HEADS)),
        'ssm_d': gain((N_EVEN, SSM_HEADS)),
        'ssm_norm_w': gain((N_EVEN, SSM_INNER)),
        'od_w_in': nrm((N_ODD, D, ODD_IN), D ** -0.5),
        'od_w_out': nrm((N_ODD, GDN_VW, D), GDN_VW ** -0.5 * BETA),
        'gdn_conv_w': nrm((N_ODD, CONV_W, GDN_CONV_CH), CONV_W ** -0.5),
        'gdn_dt_bias': dt_bias((N_ODD, GDN_HV)),
        'gdn_a_log': a_log((N_ODD, GDN_HV)),
        'gdn_norm_w': gain((N_ODD, GDN_DV)),
        'ln1_g': gain((DEPTH, D)),
        'ln1_b': nrm((DEPTH, D), 0.02),
        'ln2_g': gain((DEPTH, D)),
        'ln2_b': nrm((DEPTH, D), 0.02),
        'router_w': nrm((DEPTH, D, N_EXPERTS), D ** -0.5),
        'router_b': nrm((DEPTH, N_EXPERTS), 0.01),
        'exp_w_gate': nrm((DEPTH, N_EXPERTS, D, D_EXPERT), D ** -0.5),
        'exp_w_up': nrm((DEPTH, N_EXPERTS, D, D_EXPERT), D ** -0.5),
        'exp_w_down': nrm((DEPTH, N_EXPERTS, D_EXPERT, D), D_EXPERT ** -0.5 * BETA),
        'sh_w_gate': nrm((DEPTH, D, D_SHARED), D ** -0.5),
        'sh_w_up': nrm((DEPTH, D, D_SHARED), D ** -0.5),
        'sh_w_down': nrm((DEPTH, D_SHARED, D), D_SHARED ** -0.5 * BETA),
        'ple_proj': nrm((DEPTH, D_PLE, D), D_PLE ** -0.5),
        'ple_gate': nrm((DEPTH, D, D), D ** -0.5),
    }


def reference(x_prompt, x_sample, state_ret, state_ssm, state_ssm_conv, state_gdn, state_gdn_conv,
              p_prompt, p_sample, ev_w_in, ev_w_out, ssm_conv_w, ssm_conv_b, ssm_dt_bias, ssm_a_log, ssm_d,
              ssm_norm_w, od_w_in, od_w_out, gdn_conv_w, gdn_dt_bias, gdn_a_log, gdn_norm_w,
              ln1_g, ln1_b, ln2_g, ln2_b, router_w, router_b, exp_w_gate, exp_w_up, exp_w_down,
              sh_w_gate, sh_w_up, sh_w_down, ple_proj, ple_gate):
    bp, lp = x_prompt.shape[0], x_prompt.shape[1]
    ls = x_sample.shape[1]
    pos_p = jnp.arange(lp, dtype=jnp.int32)
    pos_s = PAST_LEN + jnp.arange(ls, dtype=jnp.int32)
    xp, xs = x_prompt, x_sample
    ret_p, ret_s, ssm_p, ssm_s, sconv_p, sconv_s = [], [], [], [], [], []
    gdn_p, gdn_s, gconv_p, gconv_s = [], [], [], []
    for i in range(DEPTH):
        j = i // 2
        if i % 2 == 0:
            w = (ev_w_in[j], ev_w_out[j], ssm_conv_w[j], ssm_conv_b[j], ssm_dt_bias[j], ssm_a_log[j], ssm_d[j], ssm_norm_w[j])
            z_ret = jnp.zeros((bp,) + state_ret.shape[2:], state_ret.dtype)
            z_ssm = jnp.zeros((bp,) + state_ssm.shape[2:], state_ssm.dtype)
            z_conv = jnp.zeros((bp,) + state_ssm_conv.shape[2:], state_ssm_conv.dtype)
            hp, r1, r2, r3 = _even_mixer(xp, pos_p, z_ret, z_ssm, z_conv, *w)
            hs, q1, q2, q3 = _even_mixer(xs, pos_s, state_ret[j], state_ssm[j], state_ssm_conv[j], *w)
            ret_p.append(r1); ssm_p.append(r2); sconv_p.append(r3)
            ret_s.append(q1); ssm_s.append(q2); sconv_s.append(q3)
        else:
            w = (od_w_in[j], od_w_out[j], gdn_conv_w[j], gdn_dt_bias[j], gdn_a_log[j], gdn_norm_w[j])
            z_gdn = jnp.zeros((bp,) + state_gdn.shape[2:], state_gdn.dtype)
            z_conv = jnp.zeros((bp,) + state_gdn_conv.shape[2:], state_gdn_conv.dtype)
            hp, r1, r2 = _odd_mixer(xp, z_gdn, z_conv, *w)
            hs, q1, q2 = _odd_mixer(xs, state_gdn[j], state_gdn_conv[j], *w)
            gdn_p.append(r1); gconv_p.append(r2)
            gdn_s.append(q1); gconv_s.append(q2)
        cw = (ln1_g[i], ln1_b[i], ln2_g[i], ln2_b[i], router_w[i], router_b[i], exp_w_gate[i], exp_w_up[i],
              exp_w_down[i], sh_w_gate[i], sh_w_up[i], sh_w_down[i], ple_proj[i], ple_gate[i])
        xp = _post_block(xp, hp, p_prompt[i], *cw)
        xs = _post_block(xs, hs, p_sample[i], *cw)
    return (xp, xs, jnp.stack(ret_p), jnp.stack(ret_s), jnp.stack(ssm_p), jnp.stack(ssm_s),
            jnp.stack(sconv_p), jnp.stack(sconv_s), jnp.stack(gdn_p), jnp.stack(gdn_s),
            jnp.stack(gconv_p), jnp.stack(gconv_s))
```

```python
import functools
import math

import jax
import jax.numpy as jnp
from jax import lax
from jax.experimental import pallas as pl
from jax.experimental.pallas import tpu as pltpu

F32 = jnp.float32
BF16 = jnp.bfloat16

D_MODEL = 2048
BATCH = 4
SEQ = 2048
DEPTH = 4
DEC_BATCH = 128
PAST_LEN = 16384
CHUNK = 64
CONV_W = 4
NORM_EPS = 1e-5
RET_HEADS = 8
RET_DK = D_MODEL // RET_HEADS
RET_DV = D_MODEL // RET_HEADS
ROPE_BASE = 10000.0
SSM_HEADS = 32
SSM_P = D_MODEL // SSM_HEADS
SSM_INNER = SSM_HEADS * SSM_P
SSM_GROUPS = 4
SSM_N = 128
SSM_CONV_CH = SSM_INNER + 2 * SSM_GROUPS * SSM_N
GDN_HK = 16
GDN_HV = 32
GDN_DK = 128
GDN_DV = 128
GDN_QK = GDN_HK * GDN_DK
GDN_VW = GDN_HV * GDN_DV
GDN_CONV_CH = 2 * GDN_QK + GDN_VW
EVEN_SPLITS = (RET_HEADS * RET_DK, RET_HEADS * RET_DK, RET_HEADS * RET_DV, RET_HEADS * RET_DV,
               SSM_INNER, SSM_CONV_CH, SSM_HEADS)
EVEN_MAIN = sum(EVEN_SPLITS[:-1])
ODD_SPLITS = (GDN_CONV_CH, GDN_VW, GDN_HV, GDN_HV)
ODD_MAIN = GDN_CONV_CH + GDN_VW
N_EXPERTS = 64
TOP_K = 8
D_EXPERT = D_MODEL // 4
ROUTED_SCALE = 2.5
D_PLE = 256
ALPHA = (2 * DEPTH) ** 0.25

N_PROMPT = BATCH * SEQ
N_TOK = N_PROMPT + DEC_BATCH
EXPERT_ROWS = 256
VMEM_LIMIT = 56 * 1024 * 1024


def _split(t, sizes):
    out, start = [], 0
    for s in sizes:
        out.append(t[..., start:start + s])
        start += s
    return out


def _dense_body(x_ref, w_ref, o_ref, wb_ref):
    @pl.when(pl.program_id(1) == 0)
    def _():
        wb_ref[...] = w_ref[...].astype(BF16)

    o_ref[...] = jnp.dot(x_ref[...].astype(BF16), wb_ref[...], preferred_element_type=F32)


def _dense(x, w, lead=(), *, n_cols=None, tm, tn):
    m, k = x.shape
    n_cols = w.shape[-1] if n_cols is None else n_cols
    assert m % tm == 0 and n_cols % tn == 0
    nl = len(lead)
    w_block = (None,) * nl + (k, tn)
    return pl.pallas_call(
        _dense_body,
        grid=(n_cols // tn, m // tm),
        in_specs=[pl.BlockSpec((tm, k), lambda j, i: (i, 0)),
                  pl.BlockSpec(w_block, lambda j, i: lead + (0, j))],
        out_specs=pl.BlockSpec((tm, tn), lambda j, i: (i, j)),
        out_shape=jax.ShapeDtypeStruct((m, n_cols), F32),
        scratch_shapes=[pltpu.VMEM((k, tn), BF16)],
        compiler_params=pltpu.CompilerParams(dimension_semantics=("arbitrary", "arbitrary"),
                                             vmem_limit_bytes=VMEM_LIMIT),
    )(x, w)


def _dense_f32_body(x_ref, w_ref, o_ref):
    o_ref[...] = jnp.dot(x_ref[...], w_ref[...], preferred_element_type=F32,
                         precision=lax.Precision.HIGHEST)


def _dense_f32(x, w, lead=(), *, tm):
    m, k = x.shape
    n = w.shape[-1]
    nl = len(lead)
    return pl.pallas_call(
        _dense_f32_body,
        grid=(m // tm,),
        in_specs=[pl.BlockSpec((tm, k), lambda i: (i, 0)),
                  pl.BlockSpec((None,) * nl + (k, n), lambda i: lead + (0, 0))],
        out_specs=pl.BlockSpec((tm, n), lambda i: (i, 0)),
        out_shape=jax.ShapeDtypeStruct((m, n), F32),
        compiler_params=pltpu.CompilerParams(dimension_semantics=("arbitrary",),
                                             vmem_limit_bytes=VMEM_LIMIT),
    )(x, w)


def _experts_body(be_ref, nu_ref, x_ref, wg_ref, wu_ref, wd_ref, y_ref, wgb, wub, wdb):
    b = pl.program_id(0)
    e = be_ref[b]
    prev = be_ref[jnp.maximum(b - 1, 0)]

    @pl.when((b == 0) | (e != prev))
    def _():
        wgb[...] = wg_ref[...].astype(BF16)
        wub[...] = wu_ref[...].astype(BF16)
        wdb[...] = wd_ref[...].astype(BF16)

    @pl.when(b < nu_ref[0])
    def _():
        xb = x_ref[...]
        g = jnp.dot(xb, wgb[...], preferred_element_type=F32)
        u = jnp.dot(xb, wub[...], preferred_element_type=F32)
        h = (g * jax.nn.sigmoid(g)) * u
        y_ref[...] = jnp.dot(h.astype(BF16), wdb[...], preferred_element_type=F32)

    @pl.when(b >= nu_ref[0])
    def _():
        y_ref[...] = jnp.zeros_like(y_ref)


def _experts(xg, block_expert, n_used, we_gate, we_up, we_down, layer):
    rows, d = xg.shape
    tm = EXPERT_ROWS
    n_blocks = rows // tm
    grid_spec = pltpu.PrefetchScalarGridSpec(
        num_scalar_prefetch=2,
        grid=(n_blocks,),
        in_specs=[pl.BlockSpec((tm, d), lambda b, be, nu: (b, 0)),
                  pl.BlockSpec((None, None, d, D_EXPERT), lambda b, be, nu: (layer, be[b], 0, 0)),
                  pl.BlockSpec((None, None, d, D_EXPERT), lambda b, be, nu: (layer, be[b], 0, 0)),
                  pl.BlockSpec((None, None, D_EXPERT, d), lambda b, be, nu: (layer, be[b], 0, 0))],
        out_specs=pl.BlockSpec((tm, d), lambda b, be, nu: (b, 0)),
        scratch_shapes=[pltpu.VMEM((d, D_EXPERT), BF16), pltpu.VMEM((d, D_EXPERT), BF16),
                        pltpu.VMEM((D_EXPERT, d), BF16)],
    )
    return pl.pallas_call(
        _experts_body,
        grid_spec=grid_spec,
        out_shape=jax.ShapeDtypeStruct((rows, d), F32),
        compiler_params=pltpu.CompilerParams(dimension_semantics=("arbitrary",),
                                             vmem_limit_bytes=VMEM_LIMIT),
    )(block_expert, n_used, xg, we_gate, we_up, we_down)


def _route_plan(idx, n_tok):
    tm = EXPERT_ROWS
    n_pairs = n_tok * TOP_K
    n_blocks = (n_pairs + N_EXPERTS * (tm - 1) + tm - 1) // tm
    flat_e = idx.reshape(-1).astype(jnp.int32)
    order = jnp.argsort(flat_e, stable=True).astype(jnp.int32)
    sorted_e = flat_e[order]
    counts = jnp.bincount(flat_e, length=N_EXPERTS).astype(jnp.int32)
    padded = (counts + tm - 1) // tm * tm
    pad_end = jnp.cumsum(padded)
    pad_start = pad_end - padded
    grp_start = jnp.cumsum(counts) - counts
    dest = pad_start[sorted_e] + jnp.arange(n_pairs, dtype=jnp.int32) - grp_start[sorted_e]
    rows_tok = jnp.full((n_blocks * tm,), n_tok, jnp.int32).at[dest].set(order // TOP_K)
    pair_row = jnp.zeros((n_pairs,), jnp.int32).at[order].set(dest)
    n_used = (pad_end[-1] // tm).astype(jnp.int32)
    blk = jnp.arange(n_blocks, dtype=jnp.int32)
    be = jnp.minimum(jnp.searchsorted(pad_end, blk * tm, side='right'), N_EXPERTS - 1).astype(jnp.int32)
    be = jnp.where(blk < n_used, be, be[jnp.maximum(n_used - 1, 0)])
    return rows_tok, pair_row.reshape(n_tok, TOP_K), be, n_used.reshape(1)


def _layer_norm(xf, g, b):
    xc = xf - jnp.mean(xf, -1, keepdims=True)
    var = jnp.mean(xc * xc, -1, keepdims=True)
    return xc * lax.rsqrt(var + NORM_EPS) * g + b


def _rms(xf):
    return xf * lax.rsqrt(jnp.mean(xf * xf, -1, keepdims=True) + NORM_EPS)


def _head_ln(xf):
    xc = xf - jnp.mean(xf, -1, keepdims=True)
    return xc * lax.rsqrt(jnp.mean(xc * xc, -1, keepdims=True) + NORM_EPS)


def _l2norm(xf):
    return xf * lax.rsqrt(jnp.sum(xf * xf, -1, keepdims=True) + 1e-6)


def _rotary(t, pos):
    half = t.shape[-1] // 2
    inv = ROPE_BASE ** (-jnp.arange(half, dtype=F32) / half)
    ang = pos.astype(F32)[:, None] * inv
    cos = jnp.cos(ang)[None, :, None, :]
    sin = jnp.sin(ang)[None, :, None, :]
    t1, t2 = t[..., :half], t[..., half:]
    return jnp.concatenate([t1 * cos - t2 * sin, t1 * sin + t2 * cos], -1)


def _chunk_len(L):
    return CHUNK if L % CHUNK == 0 else L


def _to_chunks(t, c):
    return t.reshape(t.shape[0], t.shape[1] // c, c, *t.shape[2:]).swapaxes(0, 1)


def _from_chunks(t):
    t = t.swapaxes(0, 1)
    return t.reshape(t.shape[0], t.shape[1] * t.shape[2], *t.shape[3:])


def _causal_conv(x, prev, w, b):
    L = x.shape[1]
    xp = jnp.concatenate([prev.astype(x.dtype), x], axis=1)
    y = xp[:, CONV_W - 1:] * w[CONV_W - 1]
    for i in range(CONV_W - 1):
        y = y + xp[:, i:i + L] * w[i]
    if b is not None:
        y = y + b
    return y, xp[:, L:]


def _retention_scan(q, k, v, s0, log_gamma):
    L = q.shape[1]
    c = _chunk_len(L)
    idx = jnp.arange(c, dtype=F32)
    diff = idx[:, None] - idx[None, :]
    dmask = jnp.exp(jnp.where((diff >= 0)[None], log_gamma[:, None, None] * diff[None], -jnp.inf))
    q_dec = jnp.exp((idx[:, None] + 1.0) * log_gamma)[:, :, None]
    k_dec = jnp.exp((c - 1.0 - idx)[:, None] * log_gamma)[:, :, None]
    c_dec = jnp.exp(c * log_gamma)[:, None, None]

    def step(s, inp):
        qc, kc, vc = inp
        att = jnp.einsum('bihd,bjhd->bhij', qc, kc) * dmask
        o = jnp.einsum('bhij,bjhe->bihe', att, vc) + jnp.einsum('bihd,bhde->bihe', qc * q_dec, s)
        s = s * c_dec + jnp.einsum('bjhd,bjhe->bhde', kc * k_dec, vc)
        return s, o

    s, o = lax.scan(step, s0, (_to_chunks(q, c), _to_chunks(k, c), _to_chunks(v, c)))
    return _from_chunks(o), s


def _ssd_scan(x, dt, a, bm, cm, s0):
    L = x.shape[1]
    c = _chunk_len(L)
    ar = jnp.arange(c)
    mask5 = (ar[:, None] >= ar[None, :])[None, :, :, None, None]

    def step(s, inp):
        xc, dtc, bc, cc = inp
        cs = jnp.cumsum(dtc * a, axis=1)
        lmat = jnp.exp(jnp.where(mask5, cs[:, :, None] - cs[:, None, :], -jnp.inf))
        cb = jnp.einsum('bign,bjgn->bijg', cc, bc)
        y = jnp.einsum('bijg,bijgr,bjgr,bjgrp->bigrp', cb, lmat, dtc, xc)
        y = y + jnp.einsum('bign,bgrpn->bigrp', cc, s) * jnp.exp(cs)[..., None]
        last = cs[:, -1]
        w_end = jnp.exp(last[:, None] - cs) * dtc
        s = s * jnp.exp(last)[..., None, None] + jnp.einsum('bjgr,bjgn,bjgrp->bgrpn', w_end, bc, xc)
        return s, y

    s, y = lax.scan(step, s0, (_to_chunks(x, c), _to_chunks(dt, c), _to_chunks(bm, c), _to_chunks(cm, c)))
    return _from_chunks(y), s


def _gdn_scan(q, k, v, g, beta, s0):
    L = q.shape[1]
    dv = v.shape[-1]
    c = _chunk_len(L)
    ar = jnp.arange(c)
    incl = ar[:, None] >= ar[None, :]
    strict = ar[:, None] > ar[None, :]
    eye = jnp.eye(c, dtype=F32)

    def step(s, inp):
        qc, kc, vc, gc, bc = inp
        gcum = jnp.cumsum(gc, axis=1)
        gh = gcum.swapaxes(1, 2)
        decay = jnp.exp(jnp.where(incl, gh[..., :, None] - gh[..., None, :], -jnp.inf))
        kb = kc * bc[..., None]
        a_mat = jnp.where(strict, jnp.einsum('bihd,bjhd->bhij', kb, kc) * decay, 0.0)
        rhs = jnp.concatenate([(vc * bc[..., None]).swapaxes(1, 2),
                               (kb * jnp.exp(gcum)[..., None]).swapaxes(1, 2)], -1)
        sol = lax.linalg.triangular_solve(a_mat + eye, rhs, left_side=True, lower=True, unit_diagonal=True)
        u = sol[..., :dv] - jnp.einsum('bhid,bhde->bhie', sol[..., dv:], s)
        qk = jnp.einsum('bihd,bjhd->bhij', qc, kc) * decay
        o = jnp.einsum('bihd,bhde->bihe', qc * jnp.exp(gcum)[..., None], s) + jnp.einsum('bhij,bhje->bihe', qk, u)
        g_last = gcum[:, -1]
        s = s * jnp.exp(g_last)[..., None, None] + jnp.einsum(
            'bjhd,bhje->bhde', kc * jnp.exp(g_last[:, None] - gcum)[..., None], u)
        return s, o

    xs = (_to_chunks(q, c), _to_chunks(k, c), _to_chunks(v, c), _to_chunks(g, c), _to_chunks(beta, c))
    s, o = lax.scan(step, s0, xs)
    return _from_chunks(o), s


def _even_core(proj, dt_raw, pos, s_ret, s_ssm, conv_prev, conv_w, conv_b, dt_bias, a_log, d_skip, norm_w):
    bsz, L, _ = proj.shape
    q, k, v, g, z, xbc = _split(proj, EVEN_SPLITS[:-1])
    q = _rotary(q.reshape(bsz, L, RET_HEADS, RET_DK), pos)
    k = _rotary(k.reshape(bsz, L, RET_HEADS, RET_DK), pos) * RET_DK ** -0.5
    v = v.reshape(bsz, L, RET_HEADS, RET_DV)
    log_gamma = jnp.log1p(-jnp.exp2(-5.0 - jnp.arange(RET_HEADS, dtype=F32)))
    o_ret, s_ret_new = _retention_scan(q, k, v, s_ret, log_gamma)
    o_ret = jax.nn.silu(g) * _head_ln(o_ret).reshape(bsz, L, -1)
    xbc, conv_new = _causal_conv(xbc, conv_prev, conv_w, conv_b)
    xs, bm, cm = _split(jax.nn.silu(xbc), (SSM_INNER, SSM_GROUPS * SSM_N, SSM_GROUPS * SSM_N))
    r = SSM_HEADS // SSM_GROUPS
    xs = xs.reshape(bsz, L, SSM_GROUPS, r, SSM_P)
    dt = jax.nn.softplus(dt_raw + dt_bias).reshape(bsz, L, SSM_GROUPS, r)
    a = -jnp.exp(a_log).reshape(SSM_GROUPS, r)
    y, s_ssm_new = _ssd_scan(xs, dt, a, bm.reshape(bsz, L, SSM_GROUPS, SSM_N), cm.reshape(bsz, L, SSM_GROUPS, SSM_N),
                             s_ssm.reshape(bsz, SSM_GROUPS, r, SSM_P, SSM_N))
    y = y + d_skip.reshape(SSM_GROUPS, r, 1) * xs
    y = y.reshape(bsz, L, SSM_INNER) * jax.nn.silu(z)
    y = _rms(y.reshape(bsz, L, SSM_GROUPS, -1)).reshape(bsz, L, SSM_INNER) * norm_w
    mix = jnp.concatenate([o_ret, y], -1)
    return mix, s_ret_new, s_ssm_new.reshape(bsz, SSM_HEADS, SSM_P, SSM_N), conv_new


def _odd_core(proj, ab, s_gdn, conv_prev, conv_w, dt_bias, a_log, norm_w):
    bsz, L, _ = proj.shape
    qkv, z = _split(proj, ODD_SPLITS[:2])
    a, b = _split(ab, ODD_SPLITS[2:])
    qkv, conv_new = _causal_conv(qkv, conv_prev, conv_w, None)
    q, k, v = _split(jax.nn.silu(qkv), (GDN_QK, GDN_QK, GDN_VW))
    rep = GDN_HV // GDN_HK
    q = jnp.repeat(_l2norm(q.reshape(bsz, L, GDN_HK, GDN_DK)), rep, axis=2) * GDN_DK ** -0.5
    k = jnp.repeat(_l2norm(k.reshape(bsz, L, GDN_HK, GDN_DK)), rep, axis=2)
    v = v.reshape(bsz, L, GDN_HV, GDN_DV)
    g = -jnp.exp(a_log) * jax.nn.softplus(a + dt_bias)
    beta = jax.nn.sigmoid(b)
    o, s_new = _gdn_scan(q, k, v, g, beta, s_gdn)
    o = _rms(o) * norm_w * jax.nn.silu(z.reshape(bsz, L, GDN_HV, GDN_DV))
    return o.reshape(bsz, L, GDN_VW), s_new, conv_new


def _post_block(x, mix_out, p, layer, ln1_g, ln1_b, ln2_g, ln2_b, router_w, router_b, we_gate, we_up, we_down,
                ws_gate, ws_up, ws_down, ple_proj, ple_gate):
    n_tok = x.shape[0]
    x = _layer_norm(ALPHA * x + mix_out, ln1_g[layer], ln1_b[layer])
    logits = _dense_f32(x, router_w, (layer,), tm=1040)
    scores = jax.nn.sigmoid(logits)
    _, idx = lax.top_k(scores + router_b[layer], TOP_K)
    wts = jnp.take_along_axis(scores, idx, axis=-1)
    wts = wts / jnp.sum(wts, -1, keepdims=True) * ROUTED_SCALE
    sg = _dense(x, ws_gate, (layer,), tm=1040, tn=512)
    su = _dense(x, ws_up, (layer,), tm=1040, tn=512)
    shared = _dense(jax.nn.silu(sg) * su, ws_down, (layer,), tm=1040, tn=1024)
    rows_tok, pair_row, be, n_used = _route_plan(idx, n_tok)
    x_pad = jnp.concatenate([x.astype(BF16), jnp.zeros((1, D_MODEL), BF16)], 0)
    y_rows = _experts(x_pad[rows_tok], be, n_used, we_gate, we_up, we_down, layer)
    routed = jnp.sum(y_rows[pair_row] * wts[..., None], axis=1)
    x = _layer_norm(ALPHA * x + shared + routed, ln2_g[layer], ln2_b[layer])
    gate = _dense(x, ple_gate, (layer,), tm=1040, tn=512)
    emb = _dense(p, ple_proj, (layer,), tm=1040, tn=1024)
    return x + jax.nn.sigmoid(gate) * emb


def kernel(x_prompt, x_sample, state_ret, state_ssm, state_ssm_conv, state_gdn, state_gdn_conv, p_prompt, p_sample, ev_w_in, ev_w_out, ssm_conv_w, ssm_conv_b, ssm_dt_bias, ssm_a_log, ssm_d, ssm_norm_w, od_w_in, od_w_out, gdn_conv_w, gdn_dt_bias, gdn_a_log, gdn_norm_w, ln1_g, ln1_b, ln2_g, ln2_b, router_w, router_b, exp_w_gate, exp_w_up, exp_w_down, sh_w_gate, sh_w_up, sh_w_down, ple_proj, ple_gate):
    bp, lp = x_prompt.shape[0], x_prompt.shape[1]
    ls = x_sample.shape[1]
    pos_p = jnp.arange(lp, dtype=jnp.int32)
    pos_s = PAST_LEN + jnp.arange(ls, dtype=jnp.int32)
    x = jnp.concatenate([x_prompt.reshape(N_PROMPT, D_MODEL), x_sample.reshape(DEC_BATCH, D_MODEL)], 0)
    p_all = jnp.concatenate([p_prompt.reshape(DEPTH, N_PROMPT, D_PLE), p_sample.reshape(DEPTH, DEC_BATCH, D_PLE)], 1)
    ret_p, ret_s, ssm_p, ssm_s, sconv_p, sconv_s = [], [], [], [], [], []
    gdn_p, gdn_s, gconv_p, gconv_s = [], [], [], []
    for i in range(DEPTH):
        j = i // 2
        if i % 2 == 0:
            proj = _dense(x, ev_w_in, (j,), n_cols=EVEN_MAIN, tm=1040, tn=512)
            dt_raw = _dense(x, ev_w_in[j, :, EVEN_MAIN:], tm=1040, tn=SSM_HEADS)
            w = (ssm_conv_w[j], ssm_conv_b[j], ssm_dt_bias[j], ssm_a_log[j], ssm_d[j], ssm_norm_w[j])
            z_ret = jnp.zeros((bp,) + state_ret.shape[2:], F32)
            z_ssm = jnp.zeros((bp,) + state_ssm.shape[2:], F32)
            z_conv = jnp.zeros((bp,) + state_ssm_conv.shape[2:], F32)
            mp, r1, r2, r3 = _even_core(proj[:N_PROMPT].reshape(bp, lp, -1), dt_raw[:N_PROMPT].reshape(bp, lp, -1),
                                        pos_p, z_ret, z_ssm, z_conv, *w)
            ms, q1, q2, q3 = _even_core(proj[N_PROMPT:].reshape(DEC_BATCH, ls, -1),
                                        dt_raw[N_PROMPT:].reshape(DEC_BATCH, ls, -1),
                                        pos_s, state_ret[j], state_ssm[j], state_ssm_conv[j], *w)
            ret_p.append(r1); ssm_p.append(r2); sconv_p.append(r3)
            ret_s.append(q1); ssm_s.append(q2); sconv_s.append(q3)
            mix = jnp.concatenate([mp.reshape(N_PROMPT, -1), ms.reshape(DEC_BATCH, -1)], 0)
            mix_out = _dense(mix, ev_w_out, (j,), tm=520, tn=512)
        else:
            proj = _dense(x, od_w_in, (j,), n_cols=ODD_MAIN, tm=1040, tn=512)
            ab = _dense(x, od_w_in[j, :, ODD_MAIN:], tm=1040, tn=2 * GDN_HV)
            w = (gdn_conv_w[j], gdn_dt_bias[j], gdn_a_log[j], gdn_norm_w[j])
            z_gdn = jnp.zeros((bp,) + state_gdn.shape[2:], F32)
            z_conv = jnp.zeros((bp,) + state_gdn_conv.shape[2:], F32)
            mp, r1, r2 = _odd_core(proj[:N_PROMPT].reshape(bp, lp, -1), ab[:N_PROMPT].reshape(bp, lp, -1),
                                   z_gdn, z_conv, *w)
            ms, q1, q2 = _odd_core(proj[N_PROMPT:].reshape(DEC_BATCH, ls, -1), ab[N_PROMPT:].reshape(DEC_BATCH, ls, -1),
                                   state_gdn[j], state_gdn_conv[j], *w)
            gdn_p.append(r1); gconv_p.append(r2)
            gdn_s.append(q1); gconv_s.append(q2)
            mix = jnp.concatenate([mp.reshape(N_PROMPT, -1), ms.reshape(DEC_BATCH, -1)], 0)
            mix_out = _dense(mix, od_w_out, (j,), tm=520, tn=512)
        x = _post_block(x, mix_out, p_all[i], i, ln1_g, ln1_b, ln2_g, ln2_b, router_w, router_b,
                        exp_w_gate, exp_w_up, exp_w_down, sh_w_gate, sh_w_up, sh_w_down, ple_proj, ple_gate)
    xp = x[:N_PROMPT].reshape(bp, lp, D_MODEL)
    xs = x[N_PROMPT:].reshape(DEC_BATCH, ls, D_MODEL)
    return (xp, xs, jnp.stack(ret_p), jnp.stack(ret_s), jnp.stack(ssm_p), jnp.stack(ssm_s),
            jnp.stack(sconv_p), jnp.stack(sconv_s), jnp.stack(gdn_p), jnp.stack(gdn_s),
            jnp.stack(gconv_p), jnp.stack(gconv_s))
```

```python
import functools
import math

import jax
import jax.numpy as jnp
from jax import lax
from jax.experimental import pallas as pl
from jax.experimental.pallas import tpu as pltpu

F32 = jnp.float32
BF16 = jnp.bfloat16

D_MODEL = 2048
BATCH = 4
SEQ = 2048
DEPTH = 4
DEC_BATCH = 128
PAST_LEN = 16384
CHUNK = 64
CONV_W = 4
NORM_EPS = 1e-5
RET_HEADS = 8
RET_DK = D_MODEL // RET_HEADS
RET_DV = D_MODEL // RET_HEADS
ROPE_BASE = 10000.0
SSM_HEADS = 32
SSM_P = D_MODEL // SSM_HEADS
SSM_INNER = SSM_HEADS * SSM_P
SSM_GROUPS = 4
SSM_N = 128
SSM_CONV_CH = SSM_INNER + 2 * SSM_GROUPS * SSM_N
GDN_HK = 16
GDN_HV = 32
GDN_DK = 128
GDN_DV = 128
GDN_QK = GDN_HK * GDN_DK
GDN_VW = GDN_HV * GDN_DV
GDN_CONV_CH = 2 * GDN_QK + GDN_VW
EVEN_SPLITS = (RET_HEADS * RET_DK, RET_HEADS * RET_DK, RET_HEADS * RET_DV, RET_HEADS * RET_DV,
               SSM_INNER, SSM_CONV_CH, SSM_HEADS)
EVEN_MAIN = sum(EVEN_SPLITS[:-1])
EVEN_MIX = RET_HEADS * RET_DV + SSM_INNER
ODD_SPLITS = (GDN_CONV_CH, GDN_VW, GDN_HV, GDN_HV)
ODD_MAIN = GDN_CONV_CH + GDN_VW
N_EXPERTS = 64
TOP_K = 8
D_EXPERT = D_MODEL // 4
ROUTED_SCALE = 2.5
D_PLE = 256
ALPHA = (2 * DEPTH) ** 0.25

N_PROMPT = BATCH * SEQ
N_TOK = N_PROMPT + DEC_BATCH
EXPERT_ROWS = 256
VMEM_LIMIT = 56 * 1024 * 1024


def _split(t, sizes):
    out, start = [], 0
    for s in sizes:
        out.append(t[..., start:start + s])
        start += s
    return out


def _dense_body(x_ref, w_ref, o_ref, wb_ref):
    @pl.when(pl.program_id(1) == 0)
    def _():
        wb_ref[...] = w_ref[...].astype(BF16)

    o_ref[...] = jnp.dot(x_ref[...].astype(BF16), wb_ref[...], preferred_element_type=F32)


def _dense(x, w, lead=(), *, n_cols=None, tm, tn):
    m, k = x.shape
    n_cols = w.shape[-1] if n_cols is None else n_cols
    assert m % tm == 0 and n_cols % tn == 0
    nl = len(lead)
    w_block = (None,) * nl + (k, tn)
    return pl.pallas_call(
        _dense_body,
        grid=(n_cols // tn, m // tm),
        in_specs=[pl.BlockSpec((tm, k), lambda j, i: (i, 0)),
                  pl.BlockSpec(w_block, lambda j, i: lead + (0, j))],
        out_specs=pl.BlockSpec((tm, tn), lambda j, i: (i, j)),
        out_shape=jax.ShapeDtypeStruct((m, n_cols), F32),
        scratch_shapes=[pltpu.VMEM((k, tn), BF16)],
        compiler_params=pltpu.CompilerParams(dimension_semantics=("arbitrary", "arbitrary"),
                                             vmem_limit_bytes=VMEM_LIMIT),
    )(x, w)


def _dense_f32_body(x_ref, w_ref, o_ref):
    o_ref[...] = jnp.dot(x_ref[...].astype(BF16), w_ref[...].astype(BF16), preferred_element_type=F32)


def _dense_f32(x, w, lead=(), *, tm):
    m, k = x.shape
    n = w.shape[-1]
    nl = len(lead)
    return pl.pallas_call(
        _dense_f32_body,
        grid=(m // tm,),
        in_specs=[pl.BlockSpec((tm, k), lambda i: (i, 0)),
                  pl.BlockSpec((None,) * nl + (k, n), lambda i: lead + (0, 0))],
        out_specs=pl.BlockSpec((tm, n), lambda i: (i, 0)),
        out_shape=jax.ShapeDtypeStruct((m, n), F32),
        compiler_params=pltpu.CompilerParams(dimension_semantics=("arbitrary",),
                                             vmem_limit_bytes=VMEM_LIMIT),
    )(x, w)


def _experts_body(be_ref, nu_ref, x_ref, gate_ref, wg_ref, wu_ref, wd_ref, y_ref, wgb, wub, wdb):
    b = pl.program_id(0)
    e = be_ref[b]
    prev = be_ref[jnp.maximum(b - 1, 0)]

    @pl.when((b == 0) | (e != prev))
    def _():
        wgb[...] = wg_ref[...].astype(BF16)
        wub[...] = wu_ref[...].astype(BF16)
        wdb[...] = wd_ref[...].astype(BF16)

    @pl.when(b < nu_ref[0])
    def _():
        xb = x_ref[...]
        g = jnp.dot(xb, wgb[...], preferred_element_type=F32)
        u = jnp.dot(xb, wub[...], preferred_element_type=F32)
        h = (g * jax.nn.sigmoid(g)) * u
        y_ref[...] = jnp.dot(h.astype(BF16), wdb[...], preferred_element_type=F32) * gate_ref[...]

    @pl.when(b >= nu_ref[0])
    def _():
        y_ref[...] = jnp.zeros_like(y_ref)


def _experts(xg, row_gate, block_expert, n_used, we_gate, we_up, we_down, layer):
    rows, d = xg.shape
    tm = EXPERT_ROWS
    n_blocks = rows // tm
    grid_spec = pltpu.PrefetchScalarGridSpec(
        num_scalar_prefetch=2,
        grid=(n_blocks,),
        in_specs=[pl.BlockSpec((tm, d), lambda b, be, nu: (b, 0)),
                  pl.BlockSpec((tm, 1), lambda b, be, nu: (b, 0)),
                  pl.BlockSpec((None, None, d, D_EXPERT), lambda b, be, nu: (layer, be[b], 0, 0)),
                  pl.BlockSpec((None, None, d, D_EXPERT), lambda b, be, nu: (layer, be[b], 0, 0)),
                  pl.BlockSpec((None, None, D_EXPERT, d), lambda b, be, nu: (layer, be[b], 0, 0))],
        out_specs=pl.BlockSpec((tm, d), lambda b, be, nu: (b, 0)),
        scratch_shapes=[pltpu.VMEM((d, D_EXPERT), BF16), pltpu.VMEM((d, D_EXPERT), BF16),
                        pltpu.VMEM((D_EXPERT, d), BF16)],
    )
    return pl.pallas_call(
        _experts_body,
        grid_spec=grid_spec,
        out_shape=jax.ShapeDtypeStruct((rows, d), F32),
        compiler_params=pltpu.CompilerParams(dimension_semantics=("arbitrary",),
                                             vmem_limit_bytes=VMEM_LIMIT),
    )(block_expert, n_used, xg, row_gate, we_gate, we_up, we_down)


def _route_plan(idx, wts, n_tok):
    tm = EXPERT_ROWS
    n_pairs = n_tok * TOP_K
    n_blocks = (n_pairs + N_EXPERTS * (tm - 1) + tm - 1) // tm
    flat_e = idx.reshape(-1).astype(jnp.int32)
    order = jnp.argsort(flat_e, stable=True).astype(jnp.int32)
    sorted_e = flat_e[order]
    counts = jnp.bincount(flat_e, length=N_EXPERTS).astype(jnp.int32)
    padded = (counts + tm - 1) // tm * tm
    pad_end = jnp.cumsum(padded)
    pad_start = pad_end - padded
    grp_start = jnp.cumsum(counts) - counts
    dest = pad_start[sorted_e] + jnp.arange(n_pairs, dtype=jnp.int32) - grp_start[sorted_e]
    pair_row = jnp.zeros((n_pairs,), jnp.int32).at[order].set(dest, unique_indices=True)
    n_used = (pad_end[-1] // tm).astype(jnp.int32)
    blk = jnp.arange(n_blocks, dtype=jnp.int32)
    be = jnp.minimum(jnp.searchsorted(pad_end, blk * tm, side='right'), N_EXPERTS - 1).astype(jnp.int32)
    row = jnp.arange(n_blocks * tm, dtype=jnp.int32)
    row_e = be[row // tm]
    off = row - pad_start[row_e]
    valid = (off < counts[row_e]) & (row < pad_end[-1])
    pair = order[jnp.clip(grp_start[row_e] + off, 0, n_pairs - 1)]
    rows_tok = jnp.where(valid, pair // TOP_K, n_tok)
    rows_gate = jnp.where(valid, wts.reshape(-1)[pair], 0.0)
    be = jnp.where(blk < n_used, be, be[jnp.maximum(n_used - 1, 0)])
    return rows_tok, rows_gate.reshape(-1, 1), pair_row.reshape(n_tok, TOP_K), be, n_used.reshape(1)


def _combine_body(y_ref, sh_ref, x_ref, g_ref, b_ref, o_ref):
    acc = ALPHA * x_ref[...] + sh_ref[...]
    for kk in range(TOP_K):
        acc = acc + y_ref[kk]
    xc = acc - jnp.mean(acc, -1, keepdims=True)
    var = jnp.mean(xc * xc, -1, keepdims=True)
    o_ref[...] = xc * lax.rsqrt(var + NORM_EPS) * g_ref[...] + b_ref[...]


def _combine_ln(y_k, shared, x, ln_g, ln_b, layer, *, tm):
    n_tok, d = x.shape
    return pl.pallas_call(
        _combine_body,
        grid=(n_tok // tm,),
        in_specs=[pl.BlockSpec((TOP_K, tm, d), lambda i: (0, i, 0)),
                  pl.BlockSpec((tm, d), lambda i: (i, 0)),
                  pl.BlockSpec((tm, d), lambda i: (i, 0)),
                  pl.BlockSpec((None, 1, d), lambda i: (layer, 0, 0)),
                  pl.BlockSpec((None, 1, d), lambda i: (layer, 0, 0))],
        out_specs=pl.BlockSpec((tm, d), lambda i: (i, 0)),
        out_shape=jax.ShapeDtypeStruct((n_tok, d), F32),
        compiler_params=pltpu.CompilerParams(dimension_semantics=("arbitrary",), vmem_limit_bytes=VMEM_LIMIT),
    )(y_k, shared, x, ln_g.reshape(DEPTH, 1, d), ln_b.reshape(DEPTH, 1, d))


CH = 64


def _dot(a, b):
    return jnp.dot(a.astype(BF16), b.astype(BF16), preferred_element_type=F32)


def _dot_nt(a, b):
    return lax.dot_general(a.astype(BF16), b.astype(BF16), (((1,), (1,)), ((), ())), preferred_element_type=F32)


def _dot_tn(a, b):
    return lax.dot_general(a.astype(BF16), b.astype(BF16), (((0,), (0,)), ((), ())), preferred_element_type=F32)


def _split2(a):
    hi = a.astype(BF16)
    lo = (a - hi.astype(F32)).astype(BF16)
    return hi, lo


def _dot_sel(a, sel):
    hi = a.astype(BF16)
    r = a - hi.astype(F32)
    mid = r.astype(BF16)
    lo = (r - mid.astype(F32)).astype(BF16)
    d = lambda x: jnp.dot(x, sel, preferred_element_type=F32)
    return d(hi) + d(mid) + d(lo)


def _dot_hi(a, b):
    ah, al = _split2(a)
    bh, bl = _split2(b)
    d = lambda x, y: jnp.dot(x, y, preferred_element_type=F32)
    return d(ah, bh) + d(ah, bl) + d(al, bh)


def _iota2(shape, dim):
    return lax.broadcasted_iota(jnp.int32, shape, dim)


def _decay_terms(g_row, c):
    ii = _iota2((c, c), 0)
    mm = _iota2((c, c), 1)
    l_incl = jnp.where(ii >= mm, g_row, 0.0)
    l_excl = jnp.where(mm > ii, g_row, 0.0)
    m2 = _iota2((c, 256), 0)
    j2 = _iota2((c, 256), 1)
    rhs1 = jnp.where(j2 >= 128, 1.0, jnp.where(m2 > j2, 1.0, 0.0)).astype(BF16)
    ones = jnp.ones((c, 128), BF16)
    out1 = _dot_sel(l_incl, rhs1)
    return out1[:, :c], out1[:, 128:], _dot_sel(l_excl, ones)


def _col_bcast(row, c):
    ii = _iota2((c, c), 0)
    mm = _iota2((c, c), 1)
    return _dot_sel(jnp.where(ii == mm, row, 0.0), jnp.ones((c, 128), BF16))


def _softplus(x):
    return jnp.maximum(x, 0.0) + jnp.log1p(jnp.exp(-jnp.abs(x)))


def _silu(x):
    return x * jax.nn.sigmoid(x)


def _conv_block(x_ref, e_ref, w_ref, rows):
    e_ref[8:rows + 8, :] = x_ref[...]
    w = w_ref[...]
    return (e_ref[8:rows + 8, :] * w[3:4, :] + e_ref[7:rows + 7, :] * w[2:3, :]
            + e_ref[6:rows + 6, :] * w[1:2, :] + e_ref[5:rows + 5, :] * w[0:1, :])


def _conv_carry(e_ref, rows):
    e_ref[0:8, :] = e_ref[rows:rows + 8, :]


def _ret_body(lg_ref, cd_ref, q_ref, k_ref, v_ref, g_ref, cos_ref, sin_ref, o_ref, s_ref, *, rows, dk):
    h = pl.program_id(1)
    lg = lg_ref[h]

    @pl.when(pl.program_id(2) == 0)
    def _():
        s_ref[...] = jnp.zeros_like(s_ref)

    cos = cos_ref[...]
    sin = sin_ref[...]
    half = dk // 2

    def rot(x):
        x1 = x[:, :half]
        x2 = x[:, half:]
        return jnp.concatenate([x1 * cos - x2 * sin, x1 * sin + x2 * cos], axis=-1)

    q = rot(q_ref[...])
    k = rot(k_ref[...]) * (dk ** -0.5)
    v = v_ref[...]
    c = CH
    ii = _iota2((c, c), 0)
    jj = _iota2((c, c), 1)
    dmask = jnp.where(ii >= jj, jnp.exp(lg * jnp.maximum(ii - jj, 0).astype(F32)), 0.0)
    ti = _iota2((c, 1), 0).astype(F32)
    q_dec = jnp.exp((ti + 1.0) * lg)
    k_dec = jnp.exp((c - 1.0 - ti) * lg)
    s = s_ref[...]
    for ch in range(rows // c):
        sl = slice(ch * c, (ch + 1) * c)
        qc, kc, vc = q[sl], k[sl], v[sl]
        att = _dot_nt(qc, kc) * dmask
        o = _dot(att, vc) + _dot(qc * q_dec, s)
        s = s * cd_ref[h] + _dot_tn(kc * k_dec, vc)
        oc = o - jnp.mean(o, -1, keepdims=True)
        on = oc * lax.rsqrt(jnp.mean(oc * oc, -1, keepdims=True) + NORM_EPS)
        o_ref[sl, :] = _silu(g_ref[sl, :]) * on
    s_ref[...] = s


def _retention_prompt(proj, cos, sin, log_gamma, *, out_rows, out_cols, rows=256):
    ntb = SEQ // rows
    heads, dk = RET_HEADS, RET_DK
    cdec = jnp.exp(CH * log_gamma)
    smem = pl.BlockSpec(memory_space=pltpu.SMEM)

    def blk(part):
        return pl.BlockSpec((rows, dk), lambda b, h, t: (b * ntb + t, part * heads + h))

    return pl.pallas_call(
        functools.partial(_ret_body, rows=rows, dk=dk),
        grid=(BATCH, heads, ntb),
        in_specs=[smem, smem, blk(0), blk(1), blk(2), blk(3),
                  pl.BlockSpec((rows, dk // 2), lambda b, h, t: (t, 0)),
                  pl.BlockSpec((rows, dk // 2), lambda b, h, t: (t, 0))],
        out_specs=[pl.BlockSpec((rows, dk), lambda b, h, t: (b * ntb + t, h)),
                   pl.BlockSpec((None, None, dk, dk), lambda b, h, t: (b, h, 0, 0))],
        out_shape=[jax.ShapeDtypeStruct((out_rows, out_cols), F32),
                   jax.ShapeDtypeStruct((BATCH, heads, dk, dk), F32)],
        compiler_params=pltpu.CompilerParams(dimension_semantics=("arbitrary",) * 3, vmem_limit_bytes=VMEM_LIMIT),
    )(log_gamma, cdec, proj, proj, proj, proj, cos, sin)


def _ssd_body(alog_ref, dtb_ref, dsk_ref, mix_ref, x_ref, bm_ref, cm_ref, z_ref, dt_ref,
              wx_ref, wb_ref, wc_ref, bx_ref, bb_ref, bc_ref, nw_ref,
              o_ref, st_ref, cx_ref, cb_ref, cc_ref, ex, eb, ec, s_t, *, rows, hpg, hd):
    del mix_ref
    g = pl.program_id(1)
    tb = pl.program_id(2)
    last_tb = pl.num_programs(2) - 1
    c = CH
    nch = rows // c

    @pl.when(tb == 0)
    def _():
        s_t[...] = jnp.zeros_like(s_t)
        ex[0:8, :] = jnp.zeros((8, ex.shape[1]), F32)
        eb[0:8, :] = jnp.zeros((8, eb.shape[1]), F32)
        ec[0:8, :] = jnp.zeros((8, ec.shape[1]), F32)

    xs = _silu(_conv_block(x_ref, ex, wx_ref, rows) + bx_ref[...])
    bm = _silu(_conv_block(bm_ref, eb, wb_ref, rows) + bb_ref[...])
    cm = _silu(_conv_block(cm_ref, ec, wc_ref, rows) + bc_ref[...])

    @pl.when(tb == last_tb)
    def _():
        cx_ref[...] = ex[rows + 5:rows + 8, :]
        cb_ref[...] = eb[rows + 5:rows + 8, :]
        cc_ref[...] = ec[rows + 5:rows + 8, :]

    _conv_carry(ex, rows)
    _conv_carry(eb, rows)
    _conv_carry(ec, rows)

    ii = _iota2((c, c), 0)
    jj = _iota2((c, c), 1)
    incl = ii >= jj
    nw = nw_ref[...]
    for ch in range(nch):
        sl = slice(ch * c, (ch + 1) * c)
        xc = xs[sl]
        bc = bm[sl]
        cc = cm[sl]
        cbm = _dot_nt(cc, bc)
        st = s_t[...]
        ystate = _dot(cc, st)
        ys, wxs, decs = [], [], []
        for r in range(hpg):
            h = g * hpg + r
            dt_row = _softplus(dt_ref[r, ch] + dtb_ref[h])
            a_vec = -jnp.exp(jnp.zeros((1, c), F32) + alog_ref[h])
            dmat, cs_cb, rest_cb = _decay_terms(dt_row * a_vec, c)
            dt_cb = _col_bcast(dt_row, c)
            lm = jnp.where(incl, jnp.exp(jnp.where(incl, dmat, 0.0)), 0.0)
            x_r = xc[:, r * hd:(r + 1) * hd]
            y_r = _dot(cbm * lm * dt_row, x_r) + ystate[:, r * hd:(r + 1) * hd] * jnp.exp(cs_cb[:, :hd])
            ys.append(y_r + dsk_ref[h] * x_r)
            wxs.append(x_r * (jnp.exp(rest_cb[:, :hd]) * dt_cb[:, :hd]))
            decs.append(jnp.exp(cs_cb[c - 1:c, :hd]))
        s_t[...] = st * jnp.concatenate(decs, axis=-1) + _dot_tn(bc, jnp.concatenate(wxs, axis=-1))
        y = jnp.concatenate(ys, axis=-1) * _silu(z_ref[sl, :])
        o_ref[sl, :] = y * lax.rsqrt(jnp.mean(y * y, -1, keepdims=True) + NORM_EPS) * nw

    @pl.when(tb == last_tb)
    def _():
        st_ref[...] = s_t[...].T


def _ssd_prompt(mix, proj, dt_t, conv_w, conv_b, a_log, dt_bias, d_skip, norm_w, *, rows=128):
    ntb = SEQ // rows
    nch = rows // CH
    groups, hpg, hd, n_state = SSM_GROUPS, SSM_HEADS // SSM_GROUPS, SSM_P, SSM_N
    gw = hpg * hd
    z_col = (4 * RET_HEADS * RET_DK) // gw
    x_col = z_col + SSM_INNER // gw
    b_col = (x_col * gw + SSM_INNER) // n_state
    c_col = b_col + groups
    wb_col = SSM_INNER // n_state
    wc_col = wb_col + groups
    out_col = (RET_HEADS * RET_DV) // gw
    smem = pl.BlockSpec(memory_space=pltpu.SMEM)
    in_specs = [
        smem, smem, smem,
        pl.BlockSpec(memory_space=pl.ANY),
        pl.BlockSpec((rows, gw), lambda b, g, t: (b * ntb + t, x_col + g)),
        pl.BlockSpec((rows, n_state), lambda b, g, t: (b * ntb + t, b_col + g)),
        pl.BlockSpec((rows, n_state), lambda b, g, t: (b * ntb + t, c_col + g)),
        pl.BlockSpec((rows, gw), lambda b, g, t: (b * ntb + t, z_col + g)),
        pl.BlockSpec((hpg, nch, 1, CH), lambda b, g, t: (g, b * ntb + t, 0, 0)),
        pl.BlockSpec((CONV_W, gw), lambda b, g, t: (0, g)),
        pl.BlockSpec((CONV_W, n_state), lambda b, g, t: (0, wb_col + g)),
        pl.BlockSpec((CONV_W, n_state), lambda b, g, t: (0, wc_col + g)),
        pl.BlockSpec((1, gw), lambda b, g, t: (0, g)),
        pl.BlockSpec((1, n_state), lambda b, g, t: (0, wb_col + g)),
        pl.BlockSpec((1, n_state), lambda b, g, t: (0, wc_col + g)),
        pl.BlockSpec((1, gw), lambda b, g, t: (0, g)),
    ]
    out_specs = [
        pl.BlockSpec((rows, gw), lambda b, g, t: (b * ntb + t, out_col + g)),
        pl.BlockSpec((None, gw, n_state), lambda b, g, t: (b, g, 0)),
        pl.BlockSpec((None, CONV_W - 1, gw), lambda b, g, t: (b, 0, g)),
        pl.BlockSpec((None, CONV_W - 1, n_state), lambda b, g, t: (b, 0, g)),
        pl.BlockSpec((None, CONV_W - 1, n_state), lambda b, g, t: (b, 0, g)),
    ]
    out_shape = [
        jax.ShapeDtypeStruct(mix.shape, F32),
        jax.ShapeDtypeStruct((BATCH, SSM_INNER, n_state), F32),
        jax.ShapeDtypeStruct((BATCH, CONV_W - 1, SSM_INNER), F32),
        jax.ShapeDtypeStruct((BATCH, CONV_W - 1, groups * n_state), F32),
        jax.ShapeDtypeStruct((BATCH, CONV_W - 1, groups * n_state), F32),
    ]
    scratch = [pltpu.VMEM((rows + 8, gw), F32), pltpu.VMEM((rows + 8, n_state), F32),
               pltpu.VMEM((rows + 8, n_state), F32), pltpu.VMEM((n_state, gw), F32)]
    return pl.pallas_call(
        functools.partial(_ssd_body, rows=rows, hpg=hpg, hd=hd),
        grid=(BATCH, groups, ntb),
        in_specs=in_specs, out_specs=out_specs, out_shape=out_shape, scratch_shapes=scratch,
        input_output_aliases={3: 0},
        compiler_params=pltpu.CompilerParams(dimension_semantics=("arbitrary",) * 3, vmem_limit_bytes=VMEM_LIMIT),
    )(a_log, dt_bias, d_skip, mix, proj, proj, proj, proj, dt_t, conv_w, conv_w, conv_w,
      conv_b, conv_b, conv_b, norm_w)


def _inv_unit_lower(a, c):
    ii = _iota2((c, c), 0)
    jj = _iota2((c, c), 1)
    n = -a
    p = jnp.where(ii == jj, 1.0, 0.0) + n
    for _ in range(c.bit_length() - 2):
        n = _dot_hi(n, n)
        p = p + _dot_hi(p, n)
    return p


def _gdn_body(alog_ref, dtb_ref, q_ref, k_ref, v_ref, z_ref, a_ref, b_ref, wq_ref, wk_ref, wv_ref, nw_ref,
              o_ref, s_ref, cq_ref, ck_ref, cv_ref, eq, ek, ev, *, rows, rep, dk, dv):
    hk = pl.program_id(1)
    tb = pl.program_id(2)
    last_tb = pl.num_programs(2) - 1
    c = CH
    nch = rows // c

    @pl.when(tb == 0)
    def _():
        s_ref[...] = jnp.zeros_like(s_ref)
        eq[0:8, :] = jnp.zeros((8, eq.shape[1]), F32)
        ek[0:8, :] = jnp.zeros((8, ek.shape[1]), F32)
        ev[0:8, :] = jnp.zeros((8, ev.shape[1]), F32)

    q = _silu(_conv_block(q_ref, eq, wq_ref, rows))
    k = _silu(_conv_block(k_ref, ek, wk_ref, rows))
    v = _silu(_conv_block(v_ref, ev, wv_ref, rows))

    @pl.when(tb == last_tb)
    def _():
        cq_ref[...] = eq[rows + 5:rows + 8, :]
        ck_ref[...] = ek[rows + 5:rows + 8, :]
        cv_ref[...] = ev[rows + 5:rows + 8, :]

    _conv_carry(eq, rows)
    _conv_carry(ek, rows)
    _conv_carry(ev, rows)

    q = q * lax.rsqrt(jnp.sum(q * q, -1, keepdims=True) + 1e-6) * (dk ** -0.5)
    k = k * lax.rsqrt(jnp.sum(k * k, -1, keepdims=True) + 1e-6)
    ii = _iota2((c, c), 0)
    jj = _iota2((c, c), 1)
    incl = ii >= jj
    strict = ii > jj
    nw = nw_ref[...]
    for ch in range(nch):
        sl = slice(ch * c, (ch + 1) * c)
        qc = q[sl]
        kc = k[sl]
        qk0 = _dot_nt(qc, kc)
        for j in range(rep):
            hv = hk * rep + j
            a_vec = -jnp.exp(jnp.zeros((1, c), F32) + alog_ref[hv])
            g_row = a_vec * _softplus(a_ref[j, ch] + dtb_ref[hv])
            beta_row = jax.nn.sigmoid(b_ref[j, ch])
            dmat, gcum_cb, rest_cb = _decay_terms(g_row, c)
            beta_cb = _col_bcast(beta_row, c)
            decay = jnp.where(incl, jnp.exp(jnp.where(incl, dmat, 0.0)), 0.0)
            eg = jnp.exp(gcum_cb)
            kb = kc * beta_cb
            t_inv = _inv_unit_lower(jnp.where(strict, _dot_nt(kb, kc) * decay, 0.0), c)
            rhs = jnp.concatenate([v[sl, j * dv:(j + 1) * dv] * beta_cb, kb * eg], axis=-1)
            sol = _dot_hi(t_inv, rhs)
            s = s_ref[j]
            u = sol[:, :dv] - _dot(sol[:, dv:], s)
            o = _dot(qc * eg, s) + _dot(qk0 * decay, u)
            s_ref[j] = s * eg[c - 1:c, :] + _dot_tn(kc * jnp.exp(rest_cb), u)
            zc = z_ref[sl, j * dv:(j + 1) * dv]
            o_ref[sl, j * dv:(j + 1) * dv] = (o * lax.rsqrt(jnp.mean(o * o, -1, keepdims=True) + NORM_EPS)
                                              * nw * _silu(zc))


def _gdn_prompt(proj, a_t, b_t, conv_w, a_log, dt_bias, norm_w, *, out_rows, rows=128):
    hk, hv, dk, dv = GDN_HK, GDN_HV, GDN_DK, GDN_DV
    rep = hv // hk
    ntb = SEQ // rows
    nch = rows // CH
    vw = rep * dv
    kcol = hk
    vcol = (2 * GDN_QK) // vw
    zcol = (2 * GDN_QK + GDN_VW) // vw
    smem = pl.BlockSpec(memory_space=pltpu.SMEM)
    in_specs = [
        smem, smem,
        pl.BlockSpec((rows, dk), lambda b, h, t: (b * ntb + t, h)),
        pl.BlockSpec((rows, dk), lambda b, h, t: (b * ntb + t, kcol + h)),
        pl.BlockSpec((rows, vw), lambda b, h, t: (b * ntb + t, vcol + h)),
        pl.BlockSpec((rows, vw), lambda b, h, t: (b * ntb + t, zcol + h)),
        pl.BlockSpec((rep, nch, 1, CH), lambda b, h, t: (h, b * ntb + t, 0, 0)),
        pl.BlockSpec((rep, nch, 1, CH), lambda b, h, t: (h, b * ntb + t, 0, 0)),
        pl.BlockSpec((CONV_W, dk), lambda b, h, t: (0, h)),
        pl.BlockSpec((CONV_W, dk), lambda b, h, t: (0, kcol + h)),
        pl.BlockSpec((CONV_W, vw), lambda b, h, t: (0, vcol + h)),
        pl.BlockSpec((1, dv), lambda b, h, t: (0, 0)),
    ]
    out_specs = [
        pl.BlockSpec((rows, vw), lambda b, h, t: (b * ntb + t, h)),
        pl.BlockSpec((None, rep, dk, dv), lambda b, h, t: (b, h, 0, 0)),
        pl.BlockSpec((None, CONV_W - 1, dk), lambda b, h, t: (b, 0, h)),
        pl.BlockSpec((None, CONV_W - 1, dk), lambda b, h, t: (b, 0, h)),
        pl.BlockSpec((None, CONV_W - 1, vw), lambda b, h, t: (b, 0, h)),
    ]
    out_shape = [
        jax.ShapeDtypeStruct((out_rows, GDN_VW), F32),
        jax.ShapeDtypeStruct((BATCH, hv, dk, dv), F32),
        jax.ShapeDtypeStruct((BATCH, CONV_W - 1, GDN_QK), F32),
        jax.ShapeDtypeStruct((BATCH, CONV_W - 1, GDN_QK), F32),
        jax.ShapeDtypeStruct((BATCH, CONV_W - 1, GDN_VW), F32),
    ]
    scratch = [pltpu.VMEM((rows + 8, dk), F32), pltpu.VMEM((rows + 8, dk), F32), pltpu.VMEM((rows + 8, vw), F32)]
    return pl.pallas_call(
        functools.partial(_gdn_body, rows=rows, rep=rep, dk=dk, dv=dv),
        grid=(BATCH, hk, ntb),
        in_specs=in_specs, out_specs=out_specs, out_shape=out_shape, scratch_shapes=scratch,
        compiler_params=pltpu.CompilerParams(dimension_semantics=("arbitrary",) * 3, vmem_limit_bytes=VMEM_LIMIT),
    )(a_log, dt_bias, proj, proj, proj, proj, a_t, b_t, conv_w, conv_w, conv_w, norm_w)


def _layer_norm(xf, g, b):
    xc = xf - jnp.mean(xf, -1, keepdims=True)
    var = jnp.mean(xc * xc, -1, keepdims=True)
    return xc * lax.rsqrt(var + NORM_EPS) * g + b


def _rms(xf):
    return xf * lax.rsqrt(jnp.mean(xf * xf, -1, keepdims=True) + NORM_EPS)


def _head_ln(xf):
    xc = xf - jnp.mean(xf, -1, keepdims=True)
    return xc * lax.rsqrt(jnp.mean(xc * xc, -1, keepdims=True) + NORM_EPS)


def _l2norm(xf):
    return xf * lax.rsqrt(jnp.sum(xf * xf, -1, keepdims=True) + 1e-6)


def _rotary(t, pos):
    half = t.shape[-1] // 2
    inv = ROPE_BASE ** (-jnp.arange(half, dtype=F32) / half)
    ang = pos.astype(F32)[:, None] * inv
    cos = jnp.cos(ang)[None, :, None, :]
    sin = jnp.sin(ang)[None, :, None, :]
    t1, t2 = t[..., :half], t[..., half:]
    return jnp.concatenate([t1 * cos - t2 * sin, t1 * sin + t2 * cos], -1)


def _chunk_len(L):
    return CHUNK if L % CHUNK == 0 else L


def _to_chunks(t, c):
    return t.reshape(t.shape[0], t.shape[1] // c, c, *t.shape[2:]).swapaxes(0, 1)


def _from_chunks(t):
    t = t.swapaxes(0, 1)
    return t.reshape(t.shape[0], t.shape[1] * t.shape[2], *t.shape[3:])


def _causal_conv(x, prev, w, b):
    L = x.shape[1]
    xp = jnp.concatenate([prev.astype(x.dtype), x], axis=1)
    y = xp[:, CONV_W - 1:] * w[CONV_W - 1]
    for i in range(CONV_W - 1):
        y = y + xp[:, i:i + L] * w[i]
    if b is not None:
        y = y + b
    return y, xp[:, L:]


def _retention_scan(q, k, v, s0, log_gamma):
    L = q.shape[1]
    c = _chunk_len(L)
    idx = jnp.arange(c, dtype=F32)
    diff = idx[:, None] - idx[None, :]
    dmask = jnp.exp(jnp.where((diff >= 0)[None], log_gamma[:, None, None] * diff[None], -jnp.inf))
    q_dec = jnp.exp((idx[:, None] + 1.0) * log_gamma)[:, :, None]
    k_dec = jnp.exp((c - 1.0 - idx)[:, None] * log_gamma)[:, :, None]
    c_dec = jnp.exp(c * log_gamma)[:, None, None]

    def step(s, inp):
        qc, kc, vc = inp
        att = jnp.einsum('bihd,bjhd->bhij', qc, kc) * dmask
        o = jnp.einsum('bhij,bjhe->bihe', att, vc) + jnp.einsum('bihd,bhde->bihe', qc * q_dec, s)
        s = s * c_dec + jnp.einsum('bjhd,bjhe->bhde', kc * k_dec, vc)
        return s, o

    s, o = lax.scan(step, s0, (_to_chunks(q, c), _to_chunks(k, c), _to_chunks(v, c)))
    return _from_chunks(o), s


def _ssd_scan(x, dt, a, bm, cm, s0):
    L = x.shape[1]
    c = _chunk_len(L)
    ar = jnp.arange(c)
    mask5 = (ar[:, None] >= ar[None, :])[None, :, :, None, None]

    def step(s, inp):
        xc, dtc, bc, cc = inp
        cs = jnp.cumsum(dtc * a, axis=1)
        lmat = jnp.exp(jnp.where(mask5, cs[:, :, None] - cs[:, None, :], -jnp.inf))
        cb = jnp.einsum('bign,bjgn->bijg', cc, bc)
        y = jnp.einsum('bijg,bijgr,bjgr,bjgrp->bigrp', cb, lmat, dtc, xc)
        y = y + jnp.einsum('bign,bgrpn->bigrp', cc, s) * jnp.exp(cs)[..., None]
        last = cs[:, -1]
        w_end = jnp.exp(last[:, None] - cs) * dtc
        s = s * jnp.exp(last)[..., None, None] + jnp.einsum('bjgr,bjgn,bjgrp->bgrpn', w_end, bc, xc)
        return s, y

    s, y = lax.scan(step, s0, (_to_chunks(x, c), _to_chunks(dt, c), _to_chunks(bm, c), _to_chunks(cm, c)))
    return _from_chunks(y), s


def _gdn_scan(q, k, v, g, beta, s0):
    L = q.shape[1]
    dv = v.shape[-1]
    c = _chunk_len(L)
    ar = jnp.arange(c)
    incl = ar[:, None] >= ar[None, :]
    strict = ar[:, None] > ar[None, :]
    eye = jnp.eye(c, dtype=F32)

    def step(s, inp):
        qc, kc, vc, gc, bc = inp
        gcum = jnp.cumsum(gc, axis=1)
        gh = gcum.swapaxes(1, 2)
        decay = jnp.exp(jnp.where(incl, gh[..., :, None] - gh[..., None, :], -jnp.inf))
        kb = kc * bc[..., None]
        a_mat = jnp.where(strict, jnp.einsum('bihd,bjhd->bhij', kb, kc) * decay, 0.0)
        rhs = jnp.concatenate([(vc * bc[..., None]).swapaxes(1, 2),
                               (kb * jnp.exp(gcum)[..., None]).swapaxes(1, 2)], -1)
        sol = lax.linalg.triangular_solve(a_mat + eye, rhs, left_side=True, lower=True, unit_diagonal=True)
        u = sol[..., :dv] - jnp.einsum('bhid,bhde->bhie', sol[..., dv:], s)
        qk = jnp.einsum('bihd,bjhd->bhij', qc, kc) * decay
        o = jnp.einsum('bihd,bhde->bihe', qc * jnp.exp(gcum)[..., None], s) + jnp.einsum('bhij,bhje->bihe', qk, u)
        g_last = gcum[:, -1]
        s = s * jnp.exp(g_last)[..., None, None] + jnp.einsum(
            'bjhd,bhje->bhde', kc * jnp.exp(g_last[:, None] - gcum)[..., None], u)
        return s, o

    xs = (_to_chunks(q, c), _to_chunks(k, c), _to_chunks(v, c), _to_chunks(g, c), _to_chunks(beta, c))
    s, o = lax.scan(step, s0, xs)
    return _from_chunks(o), s


def _even_core(proj, dt_raw, pos, s_ret, s_ssm, conv_prev, conv_w, conv_b, dt_bias, a_log, d_skip, norm_w):
    bsz, L, _ = proj.shape
    q, k, v, g, z, xbc = _split(proj, EVEN_SPLITS[:-1])
    q = _rotary(q.reshape(bsz, L, RET_HEADS, RET_DK), pos)
    k = _rotary(k.reshape(bsz, L, RET_HEADS, RET_DK), pos) * RET_DK ** -0.5
    v = v.reshape(bsz, L, RET_HEADS, RET_DV)
    log_gamma = jnp.log1p(-jnp.exp2(-5.0 - jnp.arange(RET_HEADS, dtype=F32)))
    o_ret, s_ret_new = _retention_scan(q, k, v, s_ret, log_gamma)
    o_ret = jax.nn.silu(g) * _head_ln(o_ret).reshape(bsz, L, -1)
    xbc, conv_new = _causal_conv(xbc, conv_prev, conv_w, conv_b)
    xs, bm, cm = _split(jax.nn.silu(xbc), (SSM_INNER, SSM_GROUPS * SSM_N, SSM_GROUPS * SSM_N))
    r = SSM_HEADS // SSM_GROUPS
    xs = xs.reshape(bsz, L, SSM_GROUPS, r, SSM_P)
    dt = jax.nn.softplus(dt_raw + dt_bias).reshape(bsz, L, SSM_GROUPS, r)
    a = -jnp.exp(a_log).reshape(SSM_GROUPS, r)
    y, s_ssm_new = _ssd_scan(xs, dt, a, bm.reshape(bsz, L, SSM_GROUPS, SSM_N), cm.reshape(bsz, L, SSM_GROUPS, SSM_N),
                             s_ssm.reshape(bsz, SSM_GROUPS, r, SSM_P, SSM_N))
    y = y + d_skip.reshape(SSM_GROUPS, r, 1) * xs
    y = y.reshape(bsz, L, SSM_INNER) * jax.nn.silu(z)
    y = _rms(y.reshape(bsz, L, SSM_GROUPS, -1)).reshape(bsz, L, SSM_INNER) * norm_w
    mix = jnp.concatenate([o_ret, y], -1)
    return mix, s_ret_new, s_ssm_new.reshape(bsz, SSM_HEADS, SSM_P, SSM_N), conv_new


def _odd_core(proj, ab, s_gdn, conv_prev, conv_w, dt_bias, a_log, norm_w):
    bsz, L, _ = proj.shape
    qkv, z = _split(proj, ODD_SPLITS[:2])
    a, b = _split(ab, ODD_SPLITS[2:])
    qkv, conv_new = _causal_conv(qkv, conv_prev, conv_w, None)
    q, k, v = _split(jax.nn.silu(qkv), (GDN_QK, GDN_QK, GDN_VW))
    rep = GDN_HV // GDN_HK
    q = jnp.repeat(_l2norm(q.reshape(bsz, L, GDN_HK, GDN_DK)), rep, axis=2) * GDN_DK ** -0.5
    k = jnp.repeat(_l2norm(k.reshape(bsz, L, GDN_HK, GDN_DK)), rep, axis=2)
    v = v.reshape(bsz, L, GDN_HV, GDN_DV)
    g = -jnp.exp(a_log) * jax.nn.softplus(a + dt_bias)
    beta = jax.nn.sigmoid(b)
    o, s_new = _gdn_scan(q, k, v, g, beta, s_gdn)
    o = _rms(o) * norm_w * jax.nn.silu(z.reshape(bsz, L, GDN_HV, GDN_DV))
    return o.reshape(bsz, L, GDN_VW), s_new, conv_new


def _post_block(x, mix_out, p, layer, ln1_g, ln1_b, ln2_g, ln2_b, router_w, router_b, we_gate, we_up, we_down,
                ws_gate, ws_up, ws_down, ple_proj, ple_gate):
    n_tok = x.shape[0]
    x = _layer_norm(ALPHA * x + mix_out, ln1_g[layer], ln1_b[layer])
    logits = _dense_f32(x, router_w, (layer,), tm=1040)
    scores = jax.nn.sigmoid(logits)
    _, idx = lax.top_k(scores + router_b[layer], TOP_K)
    wts = jnp.take_along_axis(scores, idx, axis=-1)
    wts = wts / jnp.sum(wts, -1, keepdims=True) * ROUTED_SCALE
    sg = _dense(x, ws_gate, (layer,), tm=1040, tn=512)
    su = _dense(x, ws_up, (layer,), tm=1040, tn=512)
    shared = _dense(jax.nn.silu(sg) * su, ws_down, (layer,), tm=1040, tn=1024)
    rows_tok, rows_gate, pair_row, be, n_used = _route_plan(idx, wts, n_tok)
    x_pad = jnp.concatenate([x.astype(BF16), jnp.zeros((1, D_MODEL), BF16)], 0)
    y_rows = _experts(x_pad[rows_tok], rows_gate, be, n_used, we_gate, we_up, we_down, layer)
    y_k = y_rows[pair_row.T.reshape(-1)].reshape(TOP_K, n_tok, D_MODEL)
    x = _combine_ln(y_k, shared, x, ln2_g, ln2_b, layer, tm=104)
    gate = _dense(x, ple_gate, (layer,), tm=1040, tn=512)
    emb = _dense(p, ple_proj, (layer,), tm=1040, tn=1024)
    return x + jax.nn.sigmoid(gate) * emb


def kernel(x_prompt, x_sample, state_ret, state_ssm, state_ssm_conv, state_gdn, state_gdn_conv, p_prompt, p_sample, ev_w_in, ev_w_out, ssm_conv_w, ssm_conv_b, ssm_dt_bias, ssm_a_log, ssm_d, ssm_norm_w, od_w_in, od_w_out, gdn_conv_w, gdn_dt_bias, gdn_a_log, gdn_norm_w, ln1_g, ln1_b, ln2_g, ln2_b, router_w, router_b, exp_w_gate, exp_w_up, exp_w_down, sh_w_gate, sh_w_up, sh_w_down, ple_proj, ple_gate):
    bp, lp = x_prompt.shape[0], x_prompt.shape[1]
    ls = x_sample.shape[1]
    pos_p = jnp.arange(lp, dtype=jnp.int32)
    pos_s = PAST_LEN + jnp.arange(ls, dtype=jnp.int32)
    x = jnp.concatenate([x_prompt.reshape(N_PROMPT, D_MODEL), x_sample.reshape(DEC_BATCH, D_MODEL)], 0)
    p_all = jnp.concatenate([p_prompt.reshape(DEPTH, N_PROMPT, D_PLE), p_sample.reshape(DEPTH, DEC_BATCH, D_PLE)], 1)
    ret_p, ret_s, ssm_p, ssm_s, sconv_p, sconv_s = [], [], [], [], [], []
    gdn_p, gdn_s, gconv_p, gconv_s = [], [], [], []
    half = RET_DK // 2
    ang = pos_p.astype(F32)[:, None] * (ROPE_BASE ** (-jnp.arange(half, dtype=F32) / half))
    cos_p, sin_p = jnp.cos(ang), jnp.sin(ang)
    log_gamma = jnp.log1p(-jnp.exp2(-5.0 - jnp.arange(RET_HEADS, dtype=F32)))
    for i in range(DEPTH):
        j = i // 2
        if i % 2 == 0:
            proj = _dense(x, ev_w_in, (j,), n_cols=EVEN_MAIN, tm=1040, tn=512)
            dt_raw = _dense(x, ev_w_in[j, :, EVEN_MAIN:], tm=1040, tn=SSM_HEADS)
            w = (ssm_conv_w[j], ssm_conv_b[j], ssm_dt_bias[j], ssm_a_log[j], ssm_d[j], ssm_norm_w[j])
            mix, r1 = _retention_prompt(proj, cos_p, sin_p, log_gamma, out_rows=N_TOK, out_cols=EVEN_MIX)
            dt_t = dt_raw[:N_PROMPT].T.reshape(SSM_HEADS, N_PROMPT // CH, 1, CH)
            mix, r2, cx, cb, cc = _ssd_prompt(mix, proj, dt_t, ssm_conv_w[j], ssm_conv_b[j].reshape(1, -1),
                                              ssm_a_log[j], ssm_dt_bias[j], ssm_d[j], ssm_norm_w[j].reshape(1, -1))
            r2 = r2.reshape(bp, SSM_HEADS, SSM_P, SSM_N)
            r3 = jnp.concatenate([cx, cb, cc], -1)
            ms, q1, q2, q3 = _even_core(proj[N_PROMPT:].reshape(DEC_BATCH, ls, -1),
                                        dt_raw[N_PROMPT:].reshape(DEC_BATCH, ls, -1),
                                        pos_s, state_ret[j], state_ssm[j], state_ssm_conv[j], *w)
            ret_p.append(r1); ssm_p.append(r2); sconv_p.append(r3)
            ret_s.append(q1); ssm_s.append(q2); sconv_s.append(q3)
            mix = lax.dynamic_update_slice(mix, ms.reshape(DEC_BATCH, -1), (N_PROMPT, 0))
            mix_out = _dense(mix, ev_w_out, (j,), tm=520, tn=512)
        else:
            proj = _dense(x, od_w_in, (j,), n_cols=ODD_MAIN, tm=1040, tn=512)
            ab = _dense(x, od_w_in[j, :, ODD_MAIN:], tm=1040, tn=2 * GDN_HV)
            w = (gdn_conv_w[j], gdn_dt_bias[j], gdn_a_log[j], gdn_norm_w[j])
            ab_t = ab[:N_PROMPT].T.reshape(2 * GDN_HV, N_PROMPT // CH, 1, CH)
            mix, r1, cq, ck, cv = _gdn_prompt(proj, ab_t[:GDN_HV], ab_t[GDN_HV:], gdn_conv_w[j], gdn_a_log[j],
                                              gdn_dt_bias[j], gdn_norm_w[j].reshape(1, -1), out_rows=N_TOK)
            r2 = jnp.concatenate([cq, ck, cv], -1)
            ms, q1, q2 = _odd_core(proj[N_PROMPT:].reshape(DEC_BATCH, ls, -1), ab[N_PROMPT:].reshape(DEC_BATCH, ls, -1),
                                   state_gdn[j], state_gdn_conv[j], *w)
            gdn_p.append(r1); gconv_p.append(r2)
            gdn_s.append(q1); gconv_s.append(q2)
            mix = lax.dynamic_update_slice(mix, ms.reshape(DEC_BATCH, -1), (N_PROMPT, 0))
            mix_out = _dense(mix, od_w_out, (j,), tm=520, tn=512)
        x = _post_block(x, mix_out, p_all[i], i, ln1_g, ln1_b, ln2_g, ln2_b, router_w, router_b,
                        exp_w_gate, exp_w_up, exp_w_down, sh_w_gate, sh_w_up, sh_w_down, ple_proj, ple_gate)
    xp = x[:N_PROMPT].reshape(bp, lp, D_MODEL)
    xs = x[N_PROMPT:].reshape(DEC_BATCH, ls, D_MODEL)
    return (xp, xs, jnp.stack(ret_p), jnp.stack(ret_s), jnp.stack(ssm_p), jnp.stack(ssm_s),
            jnp.stack(sconv_p), jnp.stack(sconv_s), jnp.stack(gdn_p), jnp.stack(gdn_s),
            jnp.stack(gconv_p), jnp.stack(gconv_s))
```

```python
import functools
import math

import jax
import jax.numpy as jnp
from jax import lax
from jax.experimental import pallas as pl
from jax.experimental.pallas import tpu as pltpu

F32 = jnp.float32
BF16 = jnp.bfloat16

D_MODEL = 2048
BATCH = 4
SEQ = 2048
DEPTH = 4
DEC_BATCH = 128
PAST_LEN = 16384
CHUNK = 64
CONV_W = 4
NORM_EPS = 1e-5
RET_HEADS = 8
RET_DK = D_MODEL // RET_HEADS
RET_DV = D_MODEL // RET_HEADS
ROPE_BASE = 10000.0
SSM_HEADS = 32
SSM_P = D_MODEL // SSM_HEADS
SSM_INNER = SSM_HEADS * SSM_P
SSM_GROUPS = 4
SSM_N = 128
SSM_CONV_CH = SSM_INNER + 2 * SSM_GROUPS * SSM_N
GDN_HK = 16
GDN_HV = 32
GDN_DK = 128
GDN_DV = 128
GDN_QK = GDN_HK * GDN_DK
GDN_VW = GDN_HV * GDN_DV
GDN_CONV_CH = 2 * GDN_QK + GDN_VW
EVEN_SPLITS = (RET_HEADS * RET_DK, RET_HEADS * RET_DK, RET_HEADS * RET_DV, RET_HEADS * RET_DV,
               SSM_INNER, SSM_CONV_CH, SSM_HEADS)
EVEN_MAIN = sum(EVEN_SPLITS[:-1])
EVEN_MIX = RET_HEADS * RET_DV + SSM_INNER
ODD_SPLITS = (GDN_CONV_CH, GDN_VW, GDN_HV, GDN_HV)
ODD_MAIN = GDN_CONV_CH + GDN_VW
N_EXPERTS = 64
TOP_K = 8
D_EXPERT = D_MODEL // 4
ROUTED_SCALE = 2.5
D_PLE = 256
ALPHA = (2 * DEPTH) ** 0.25

N_PROMPT = BATCH * SEQ
N_TOK = N_PROMPT + DEC_BATCH
EXPERT_ROWS = 256
VMEM_LIMIT = 56 * 1024 * 1024


def _split(t, sizes):
    out, start = [], 0
    for s in sizes:
        out.append(t[..., start:start + s])
        start += s
    return out


def _dense_body(x_ref, w_ref, o_ref, wb_ref):
    @pl.when(pl.program_id(1) == 0)
    def _():
        wb_ref[...] = w_ref[...].astype(BF16)

    o_ref[...] = jnp.dot(x_ref[...].astype(BF16), wb_ref[...], preferred_element_type=F32)


def _dense(x, w, lead=(), *, n_cols=None, tm, tn):
    m, k = x.shape
    n_cols = w.shape[-1] if n_cols is None else n_cols
    assert m % tm == 0 and n_cols % tn == 0
    nl = len(lead)
    w_block = (None,) * nl + (k, tn)
    return pl.pallas_call(
        _dense_body,
        grid=(n_cols // tn, m // tm),
        in_specs=[pl.BlockSpec((tm, k), lambda j, i: (i, 0)),
                  pl.BlockSpec(w_block, lambda j, i: lead + (0, j))],
        out_specs=pl.BlockSpec((tm, tn), lambda j, i: (i, j)),
        out_shape=jax.ShapeDtypeStruct((m, n_cols), F32),
        scratch_shapes=[pltpu.VMEM((k, tn), BF16)],
        compiler_params=pltpu.CompilerParams(dimension_semantics=("arbitrary", "arbitrary"),
                                             vmem_limit_bytes=VMEM_LIMIT),
    )(x, w)


def _dense_f32_body(x_ref, w_ref, o_ref):
    o_ref[...] = jnp.dot(x_ref[...].astype(BF16), w_ref[...].astype(BF16), preferred_element_type=F32)


def _dense_f32(x, w, lead=(), *, tm):
    m, k = x.shape
    n = w.shape[-1]
    nl = len(lead)
    return pl.pallas_call(
        _dense_f32_body,
        grid=(m // tm,),
        in_specs=[pl.BlockSpec((tm, k), lambda i: (i, 0)),
                  pl.BlockSpec((None,) * nl + (k, n), lambda i: lead + (0, 0))],
        out_specs=pl.BlockSpec((tm, n), lambda i: (i, 0)),
        out_shape=jax.ShapeDtypeStruct((m, n), F32),
        compiler_params=pltpu.CompilerParams(dimension_semantics=("arbitrary",),
                                             vmem_limit_bytes=VMEM_LIMIT),
    )(x, w)


def _ln_router_body(x_ref, m_ref, g_ref, b_ref, rw_ref, rb_ref, x1_ref, xb_ref, idx_ref, wts_ref):
    acc = ALPHA * x_ref[...] + m_ref[...]
    xc = acc - jnp.mean(acc, -1, keepdims=True)
    var = jnp.mean(xc * xc, -1, keepdims=True)
    x1 = xc * lax.rsqrt(var + NORM_EPS) * g_ref[...] + b_ref[...]
    x1_ref[...] = x1
    xb = x1.astype(BF16)
    xb_ref[...] = xb
    scores = jax.nn.sigmoid(jnp.dot(xb, rw_ref[...].astype(BF16), preferred_element_type=F32))
    sel = scores + rb_ref[...]
    tm, ne = sel.shape
    lane = _iota2((tm, ne), 1).astype(F32)
    slot = _iota2((tm, TOP_K), 1)
    idx = jnp.zeros((tm, TOP_K), F32)
    wts = jnp.zeros((tm, TOP_K), F32)
    for r in range(TOP_K):
        best = jnp.max(sel, axis=-1, keepdims=True)
        arg = jnp.min(jnp.where(sel == best, lane, float(ne)), axis=-1, keepdims=True)
        pick = lane == arg
        idx = jnp.where(slot == r, arg, idx)
        wts = jnp.where(slot == r, jnp.sum(jnp.where(pick, scores, 0.0), axis=-1, keepdims=True), wts)
        sel = jnp.where(pick, -jnp.inf, sel)
    idx_ref[...] = idx.astype(jnp.int32)
    wts_ref[...] = wts / jnp.sum(wts, -1, keepdims=True) * ROUTED_SCALE


def _ln_router(x, mix_out, ln_g, ln_b, router_w, router_b, layer, *, tm):
    n_tok, d = x.shape
    row = lambda w: pl.BlockSpec((tm, w), lambda i: (i, 0))
    par = lambda *shape: pl.BlockSpec((None,) + shape, lambda i: (layer,) + (0,) * len(shape))
    return pl.pallas_call(
        _ln_router_body,
        grid=(n_tok // tm,),
        in_specs=[row(d), row(d), par(1, d), par(1, d), par(d, N_EXPERTS), par(1, N_EXPERTS)],
        out_specs=[row(d), row(d), row(TOP_K), row(TOP_K)],
        out_shape=[jax.ShapeDtypeStruct((n_tok, d), F32), jax.ShapeDtypeStruct((n_tok, d), BF16),
                   jax.ShapeDtypeStruct((n_tok, TOP_K), jnp.int32), jax.ShapeDtypeStruct((n_tok, TOP_K), F32)],
        compiler_params=pltpu.CompilerParams(dimension_semantics=("arbitrary",), vmem_limit_bytes=VMEM_LIMIT),
    )(x, mix_out, ln_g.reshape(DEPTH, 1, d), ln_b.reshape(DEPTH, 1, d), router_w, router_b.reshape(DEPTH, 1, N_EXPERTS))


def _shared_body(x_ref, wg_ref, wu_ref, wd_ref, o_ref, wgb, wub, wdb):
    @pl.when(pl.program_id(0) == 0)
    def _():
        wgb[...] = wg_ref[...].astype(BF16)
        wub[...] = wu_ref[...].astype(BF16)
        wdb[...] = wd_ref[...].astype(BF16)

    xb = x_ref[...]
    g = jnp.dot(xb, wgb[...], preferred_element_type=F32)
    u = jnp.dot(xb, wub[...], preferred_element_type=F32)
    h = (g * jax.nn.sigmoid(g)) * u
    o_ref[...] = jnp.dot(h.astype(BF16), wdb[...], preferred_element_type=F32)


def _shared_expert(xb, ws_gate, ws_up, ws_down, layer, *, tm):
    n_tok, d = xb.shape
    ds = ws_gate.shape[-1]
    return pl.pallas_call(
        _shared_body,
        grid=(n_tok // tm,),
        in_specs=[pl.BlockSpec((tm, d), lambda i: (i, 0)),
                  pl.BlockSpec((None, d, ds), lambda i: (layer, 0, 0)),
                  pl.BlockSpec((None, d, ds), lambda i: (layer, 0, 0)),
                  pl.BlockSpec((None, ds, d), lambda i: (layer, 0, 0))],
        out_specs=pl.BlockSpec((tm, d), lambda i: (i, 0)),
        out_shape=jax.ShapeDtypeStruct((n_tok, d), F32),
        scratch_shapes=[pltpu.VMEM((d, ds), BF16), pltpu.VMEM((d, ds), BF16), pltpu.VMEM((ds, d), BF16)],
        compiler_params=pltpu.CompilerParams(dimension_semantics=("arbitrary",), vmem_limit_bytes=VMEM_LIMIT),
    )(xb, ws_gate, ws_up, ws_down)


def _ple_body(x_ref, xc_ref, p_ref, wg_ref, wp_ref, o_ref, wgb, wpb):
    @pl.when(pl.program_id(1) == 0)
    def _():
        wgb[...] = wg_ref[...].astype(BF16)
        wpb[...] = wp_ref[...].astype(BF16)

    gate = jnp.dot(x_ref[...].astype(BF16), wgb[...], preferred_element_type=F32)
    emb = jnp.dot(p_ref[...].astype(BF16), wpb[...], preferred_element_type=F32)
    o_ref[...] = xc_ref[...] + jax.nn.sigmoid(gate) * emb


def _ple(x, p, ple_gate, ple_proj, layer, *, tm, tn):
    n_tok, d = x.shape
    dp = p.shape[-1]
    return pl.pallas_call(
        _ple_body,
        grid=(d // tn, n_tok // tm),
        in_specs=[pl.BlockSpec((tm, d), lambda j, i: (i, 0)),
                  pl.BlockSpec((tm, tn), lambda j, i: (i, j)),
                  pl.BlockSpec((None, tm, dp), lambda j, i: (layer, i, 0)),
                  pl.BlockSpec((None, d, tn), lambda j, i: (layer, 0, j)),
                  pl.BlockSpec((None, dp, tn), lambda j, i: (layer, 0, j))],
        out_specs=pl.BlockSpec((tm, tn), lambda j, i: (i, j)),
        out_shape=jax.ShapeDtypeStruct((n_tok, d), F32),
        scratch_shapes=[pltpu.VMEM((d, tn), BF16), pltpu.VMEM((dp, tn), BF16)],
        compiler_params=pltpu.CompilerParams(dimension_semantics=("arbitrary", "arbitrary"),
                                             vmem_limit_bytes=VMEM_LIMIT),
    )(x, x, p, ple_gate, ple_proj)


def _experts_body(be_ref, nu_ref, x_ref, gate_ref, wg_ref, wu_ref, wd_ref, y_ref, wgb, wub, wdb):
    b = pl.program_id(0)
    e = be_ref[b]
    prev = be_ref[jnp.maximum(b - 1, 0)]

    @pl.when((b == 0) | (e != prev))
    def _():
        wgb[...] = wg_ref[...].astype(BF16)
        wub[...] = wu_ref[...].astype(BF16)
        wdb[...] = wd_ref[...].astype(BF16)

    @pl.when(b < nu_ref[0])
    def _():
        xb = x_ref[...]
        g = jnp.dot(xb, wgb[...], preferred_element_type=F32)
        u = jnp.dot(xb, wub[...], preferred_element_type=F32)
        h = (g * jax.nn.sigmoid(g)) * u
        y_ref[...] = jnp.dot(h.astype(BF16), wdb[...], preferred_element_type=F32) * gate_ref[...]

    @pl.when(b >= nu_ref[0])
    def _():
        y_ref[...] = jnp.zeros_like(y_ref)


def _experts(xg, row_gate, block_expert, n_used, we_gate, we_up, we_down, layer):
    rows, d = xg.shape
    tm = EXPERT_ROWS
    n_blocks = rows // tm
    grid_spec = pltpu.PrefetchScalarGridSpec(
        num_scalar_prefetch=2,
        grid=(n_blocks,),
        in_specs=[pl.BlockSpec((tm, d), lambda b, be, nu: (b, 0)),
                  pl.BlockSpec((tm, 1), lambda b, be, nu: (b, 0)),
                  pl.BlockSpec((None, None, d, D_EXPERT), lambda b, be, nu: (layer, be[b], 0, 0)),
                  pl.BlockSpec((None, None, d, D_EXPERT), lambda b, be, nu: (layer, be[b], 0, 0)),
                  pl.BlockSpec((None, None, D_EXPERT, d), lambda b, be, nu: (layer, be[b], 0, 0))],
        out_specs=pl.BlockSpec((tm, d), lambda b, be, nu: (b, 0)),
        scratch_shapes=[pltpu.VMEM((d, D_EXPERT), BF16), pltpu.VMEM((d, D_EXPERT), BF16),
                        pltpu.VMEM((D_EXPERT, d), BF16)],
    )
    return pl.pallas_call(
        _experts_body,
        grid_spec=grid_spec,
        out_shape=jax.ShapeDtypeStruct((rows, d), F32),
        compiler_params=pltpu.CompilerParams(dimension_semantics=("arbitrary",),
                                             vmem_limit_bytes=VMEM_LIMIT),
    )(block_expert, n_used, xg, row_gate, we_gate, we_up, we_down)


def _route_plan(idx, wts, n_tok):
    tm = EXPERT_ROWS
    n_pairs = n_tok * TOP_K
    n_blocks = (n_pairs + N_EXPERTS * (tm - 1) + tm - 1) // tm
    flat_e = idx.reshape(-1).astype(jnp.int32)
    order = jnp.argsort(flat_e, stable=True).astype(jnp.int32)
    sorted_e = flat_e[order]
    counts = jnp.bincount(flat_e, length=N_EXPERTS).astype(jnp.int32)
    padded = (counts + tm - 1) // tm * tm
    pad_end = jnp.cumsum(padded)
    pad_start = pad_end - padded
    grp_start = jnp.cumsum(counts) - counts
    dest = pad_start[sorted_e] + jnp.arange(n_pairs, dtype=jnp.int32) - grp_start[sorted_e]
    pair_row = jnp.zeros((n_pairs,), jnp.int32).at[order].set(dest, unique_indices=True)
    n_used = (pad_end[-1] // tm).astype(jnp.int32)
    blk = jnp.arange(n_blocks, dtype=jnp.int32)
    be = jnp.minimum(jnp.searchsorted(pad_end, blk * tm, side='right'), N_EXPERTS - 1).astype(jnp.int32)
    row = jnp.arange(n_blocks * tm, dtype=jnp.int32)
    row_e = be[row // tm]
    off = row - pad_start[row_e]
    valid = (off < counts[row_e]) & (row < pad_end[-1])
    pair = order[jnp.clip(grp_start[row_e] + off, 0, n_pairs - 1)]
    rows_tok = jnp.where(valid, pair // TOP_K, n_tok)
    rows_gate = jnp.where(valid, wts.reshape(-1)[pair], 0.0)
    be = jnp.where(blk < n_used, be, be[jnp.maximum(n_used - 1, 0)])
    return rows_tok, rows_gate.reshape(-1, 1), pair_row.reshape(n_tok, TOP_K), be, n_used.reshape(1)


def _combine_body(y_ref, sh_ref, x_ref, g_ref, b_ref, o_ref):
    acc = ALPHA * x_ref[...] + sh_ref[...]
    for kk in range(TOP_K):
        acc = acc + y_ref[kk]
    xc = acc - jnp.mean(acc, -1, keepdims=True)
    var = jnp.mean(xc * xc, -1, keepdims=True)
    o_ref[...] = xc * lax.rsqrt(var + NORM_EPS) * g_ref[...] + b_ref[...]


def _combine_ln(y_k, shared, x, ln_g, ln_b, layer, *, tm):
    n_tok, d = x.shape
    return pl.pallas_call(
        _combine_body,
        grid=(n_tok // tm,),
        in_specs=[pl.BlockSpec((TOP_K, tm, d), lambda i: (0, i, 0)),
                  pl.BlockSpec((tm, d), lambda i: (i, 0)),
                  pl.BlockSpec((tm, d), lambda i: (i, 0)),
                  pl.BlockSpec((None, 1, d), lambda i: (layer, 0, 0)),
                  pl.BlockSpec((None, 1, d), lambda i: (layer, 0, 0))],
        out_specs=pl.BlockSpec((tm, d), lambda i: (i, 0)),
        out_shape=jax.ShapeDtypeStruct((n_tok, d), F32),
        compiler_params=pltpu.CompilerParams(dimension_semantics=("arbitrary",), vmem_limit_bytes=VMEM_LIMIT),
    )(y_k, shared, x, ln_g.reshape(DEPTH, 1, d), ln_b.reshape(DEPTH, 1, d))


CH = 64


def _dot(a, b):
    return jnp.dot(a.astype(BF16), b.astype(BF16), preferred_element_type=F32)


def _dot_nt(a, b):
    return lax.dot_general(a.astype(BF16), b.astype(BF16), (((1,), (1,)), ((), ())), preferred_element_type=F32)


def _dot_tn(a, b):
    return lax.dot_general(a.astype(BF16), b.astype(BF16), (((0,), (0,)), ((), ())), preferred_element_type=F32)


def _split2(a):
    hi = a.astype(BF16)
    lo = (a - hi.astype(F32)).astype(BF16)
    return hi, lo


def _dot_sel(a, sel):
    hi = a.astype(BF16)
    r = a - hi.astype(F32)
    mid = r.astype(BF16)
    lo = (r - mid.astype(F32)).astype(BF16)
    d = lambda x: jnp.dot(x, sel, preferred_element_type=F32)
    return d(hi) + d(mid) + d(lo)


def _dot_hi(a, b):
    ah, al = _split2(a)
    bh, bl = _split2(b)
    d = lambda x, y: jnp.dot(x, y, preferred_element_type=F32)
    return d(ah, bh) + d(ah, bl) + d(al, bh)


def _iota2(shape, dim):
    return lax.broadcasted_iota(jnp.int32, shape, dim)


def _decay_terms(g_row, c):
    ii = _iota2((c, c), 0)
    mm = _iota2((c, c), 1)
    l_incl = jnp.where(ii >= mm, g_row, 0.0)
    l_excl = jnp.where(mm > ii, g_row, 0.0)
    m2 = _iota2((c, 256), 0)
    j2 = _iota2((c, 256), 1)
    rhs1 = jnp.where(j2 >= 128, 1.0, jnp.where(m2 > j2, 1.0, 0.0)).astype(BF16)
    ones = jnp.ones((c, 128), BF16)
    out1 = _dot_sel(l_incl, rhs1)
    return out1[:, :c], out1[:, 128:], _dot_sel(l_excl, ones)


def _col_bcast(row, c):
    ii = _iota2((c, c), 0)
    mm = _iota2((c, c), 1)
    return _dot_sel(jnp.where(ii == mm, row, 0.0), jnp.ones((c, 128), BF16))


def _softplus(x):
    return jnp.maximum(x, 0.0) + jnp.log1p(jnp.exp(-jnp.abs(x)))


def _silu(x):
    return x * jax.nn.sigmoid(x)


def _conv_block(x_ref, e_ref, w_ref, rows):
    e_ref[8:rows + 8, :] = x_ref[...]
    w = w_ref[...]
    return (e_ref[8:rows + 8, :] * w[3:4, :] + e_ref[7:rows + 7, :] * w[2:3, :]
            + e_ref[6:rows + 6, :] * w[1:2, :] + e_ref[5:rows + 5, :] * w[0:1, :])


def _conv_carry(e_ref, rows):
    e_ref[0:8, :] = e_ref[rows:rows + 8, :]


def _ret_body(lg_ref, cd_ref, mix_ref, q_ref, k_ref, v_ref, g_ref, cos_ref, sin_ref, o_ref, s_ref, *, rows, dk):
    del mix_ref
    h = pl.program_id(1)
    lg = lg_ref[h]

    @pl.when(pl.program_id(2) == 0)
    def _():
        s_ref[...] = jnp.zeros_like(s_ref)

    cos = cos_ref[...]
    sin = sin_ref[...]
    half = dk // 2

    def rot(x):
        x1 = x[:, :half]
        x2 = x[:, half:]
        return jnp.concatenate([x1 * cos - x2 * sin, x1 * sin + x2 * cos], axis=-1)

    q = rot(q_ref[...])
    k = rot(k_ref[...]) * (dk ** -0.5)
    v = v_ref[...]
    c = CH
    ii = _iota2((c, c), 0)
    jj = _iota2((c, c), 1)
    dmask = jnp.where(ii >= jj, jnp.exp(lg * jnp.maximum(ii - jj, 0).astype(F32)), 0.0)
    ti = _iota2((c, 1), 0).astype(F32)
    q_dec = jnp.exp((ti + 1.0) * lg)
    k_dec = jnp.exp((c - 1.0 - ti) * lg)
    s = s_ref[...]
    for ch in range(rows // c):
        sl = slice(ch * c, (ch + 1) * c)
        qc, kc, vc = q[sl], k[sl], v[sl]
        att = _dot_nt(qc, kc) * dmask
        o = _dot(att, vc) + _dot(qc * q_dec, s)
        s = s * cd_ref[h] + _dot_tn(kc * k_dec, vc)
        oc = o - jnp.mean(o, -1, keepdims=True)
        on = oc * lax.rsqrt(jnp.mean(oc * oc, -1, keepdims=True) + NORM_EPS)
        o_ref[sl, :] = _silu(g_ref[sl, :]) * on
    s_ref[...] = s


def _retention_prompt(mix, proj, cos, sin, log_gamma, *, rows=256):
    ntb = SEQ // rows
    heads, dk = RET_HEADS, RET_DK
    cdec = jnp.exp(CH * log_gamma)
    smem = pl.BlockSpec(memory_space=pltpu.SMEM)

    def blk(part):
        return pl.BlockSpec((rows, dk), lambda b, h, t: (b * ntb + t, part * heads + h))

    return pl.pallas_call(
        functools.partial(_ret_body, rows=rows, dk=dk),
        grid=(BATCH, heads, ntb),
        in_specs=[smem, smem, pl.BlockSpec(memory_space=pl.ANY), blk(0), blk(1), blk(2), blk(3),
                  pl.BlockSpec((rows, dk // 2), lambda b, h, t: (t, 0)),
                  pl.BlockSpec((rows, dk // 2), lambda b, h, t: (t, 0))],
        out_specs=[pl.BlockSpec((rows, dk), lambda b, h, t: (b * ntb + t, h)),
                   pl.BlockSpec((None, None, dk, dk), lambda b, h, t: (b, h, 0, 0))],
        out_shape=[jax.ShapeDtypeStruct(mix.shape, F32),
                   jax.ShapeDtypeStruct((BATCH, heads, dk, dk), F32)],
        input_output_aliases={2: 0},
        compiler_params=pltpu.CompilerParams(dimension_semantics=("arbitrary",) * 3, vmem_limit_bytes=VMEM_LIMIT),
    )(log_gamma, cdec, mix, proj, proj, proj, proj, cos, sin)


def _ssd_body(alog_ref, dtb_ref, dsk_ref, mix_ref, x_ref, bm_ref, cm_ref, z_ref, dt_ref,
              wx_ref, wb_ref, wc_ref, bx_ref, bb_ref, bc_ref, nw_ref,
              o_ref, st_ref, cx_ref, cb_ref, cc_ref, ex, eb, ec, s_t, *, rows, hpg, hd):
    del mix_ref
    g = pl.program_id(1)
    tb = pl.program_id(2)
    last_tb = pl.num_programs(2) - 1
    c = CH
    nch = rows // c

    @pl.when(tb == 0)
    def _():
        s_t[...] = jnp.zeros_like(s_t)
        ex[0:8, :] = jnp.zeros((8, ex.shape[1]), F32)
        eb[0:8, :] = jnp.zeros((8, eb.shape[1]), F32)
        ec[0:8, :] = jnp.zeros((8, ec.shape[1]), F32)

    xs = _silu(_conv_block(x_ref, ex, wx_ref, rows) + bx_ref[...])
    bm = _silu(_conv_block(bm_ref, eb, wb_ref, rows) + bb_ref[...])
    cm = _silu(_conv_block(cm_ref, ec, wc_ref, rows) + bc_ref[...])

    @pl.when(tb == last_tb)
    def _():
        cx_ref[...] = ex[rows + 5:rows + 8, :]
        cb_ref[...] = eb[rows + 5:rows + 8, :]
        cc_ref[...] = ec[rows + 5:rows + 8, :]

    _conv_carry(ex, rows)
    _conv_carry(eb, rows)
    _conv_carry(ec, rows)

    ii = _iota2((c, c), 0)
    jj = _iota2((c, c), 1)
    incl = ii >= jj
    nw = nw_ref[...]
    for ch in range(nch):
        sl = slice(ch * c, (ch + 1) * c)
        xc = xs[sl]
        bc = bm[sl]
        cc = cm[sl]
        cbm = _dot_nt(cc, bc)
        st = s_t[...]
        ystate = _dot(cc, st)
        ys, wxs, decs = [], [], []
        for r in range(hpg):
            h = g * hpg + r
            dt_row = _softplus(dt_ref[r, ch] + dtb_ref[h])
            a_vec = -jnp.exp(jnp.zeros((1, c), F32) + alog_ref[h])
            dmat, cs_cb, rest_cb = _decay_terms(dt_row * a_vec, c)
            dt_cb = _col_bcast(dt_row, c)
            lm = jnp.where(incl, jnp.exp(jnp.where(incl, dmat, 0.0)), 0.0)
            x_r = xc[:, r * hd:(r + 1) * hd]
            y_r = _dot(cbm * lm * dt_row, x_r) + ystate[:, r * hd:(r + 1) * hd] * jnp.exp(cs_cb[:, :hd])
            ys.append(y_r + dsk_ref[h] * x_r)
            wxs.append(x_r * (jnp.exp(rest_cb[:, :hd]) * dt_cb[:, :hd]))
            decs.append(jnp.exp(cs_cb[c - 1:c, :hd]))
        s_t[...] = st * jnp.concatenate(decs, axis=-1) + _dot_tn(bc, jnp.concatenate(wxs, axis=-1))
        y = jnp.concatenate(ys, axis=-1) * _silu(z_ref[sl, :])
        o_ref[sl, :] = y * lax.rsqrt(jnp.mean(y * y, -1, keepdims=True) + NORM_EPS) * nw

    @pl.when(tb == last_tb)
    def _():
        st_ref[...] = s_t[...].T


def _ssd_prompt(mix, proj, dt_t, conv_w, conv_b, a_log, dt_bias, d_skip, norm_w, *, rows=128):
    ntb = SEQ // rows
    nch = rows // CH
    groups, hpg, hd, n_state = SSM_GROUPS, SSM_HEADS // SSM_GROUPS, SSM_P, SSM_N
    gw = hpg * hd
    z_col = (4 * RET_HEADS * RET_DK) // gw
    x_col = z_col + SSM_INNER // gw
    b_col = (x_col * gw + SSM_INNER) // n_state
    c_col = b_col + groups
    wb_col = SSM_INNER // n_state
    wc_col = wb_col + groups
    out_col = (RET_HEADS * RET_DV) // gw
    smem = pl.BlockSpec(memory_space=pltpu.SMEM)
    in_specs = [
        smem, smem, smem,
        pl.BlockSpec(memory_space=pl.ANY),
        pl.BlockSpec((rows, gw), lambda b, g, t: (b * ntb + t, x_col + g)),
        pl.BlockSpec((rows, n_state), lambda b, g, t: (b * ntb + t, b_col + g)),
        pl.BlockSpec((rows, n_state), lambda b, g, t: (b * ntb + t, c_col + g)),
        pl.BlockSpec((rows, gw), lambda b, g, t: (b * ntb + t, z_col + g)),
        pl.BlockSpec((hpg, nch, 1, CH), lambda b, g, t: (g, b * ntb + t, 0, 0)),
        pl.BlockSpec((CONV_W, gw), lambda b, g, t: (0, g)),
        pl.BlockSpec((CONV_W, n_state), lambda b, g, t: (0, wb_col + g)),
        pl.BlockSpec((CONV_W, n_state), lambda b, g, t: (0, wc_col + g)),
        pl.BlockSpec((1, gw), lambda b, g, t: (0, g)),
        pl.BlockSpec((1, n_state), lambda b, g, t: (0, wb_col + g)),
        pl.BlockSpec((1, n_state), lambda b, g, t: (0, wc_col + g)),
        pl.BlockSpec((1, gw), lambda b, g, t: (0, g)),
    ]
    out_specs = [
        pl.BlockSpec((rows, gw), lambda b, g, t: (b * ntb + t, out_col + g)),
        pl.BlockSpec((None, gw, n_state), lambda b, g, t: (b, g, 0)),
        pl.BlockSpec((None, CONV_W - 1, gw), lambda b, g, t: (b, 0, g)),
        pl.BlockSpec((None, CONV_W - 1, n_state), lambda b, g, t: (b, 0, g)),
        pl.BlockSpec((None, CONV_W - 1, n_state), lambda b, g, t: (b, 0, g)),
    ]
    out_shape = [
        jax.ShapeDtypeStruct(mix.shape, F32),
        jax.ShapeDtypeStruct((BATCH, SSM_INNER, n_state), F32),
        jax.ShapeDtypeStruct((BATCH, CONV_W - 1, SSM_INNER), F32),
        jax.ShapeDtypeStruct((BATCH, CONV_W - 1, groups * n_state), F32),
        jax.ShapeDtypeStruct((BATCH, CONV_W - 1, groups * n_state), F32),
    ]
    scratch = [pltpu.VMEM((rows + 8, gw), F32), pltpu.VMEM((rows + 8, n_state), F32),
               pltpu.VMEM((rows + 8, n_state), F32), pltpu.VMEM((n_state, gw), F32)]
    return pl.pallas_call(
        functools.partial(_ssd_body, rows=rows, hpg=hpg, hd=hd),
        grid=(BATCH, groups, ntb),
        in_specs=in_specs, out_specs=out_specs, out_shape=out_shape, scratch_shapes=scratch,
        input_output_aliases={3: 0},
        compiler_params=pltpu.CompilerParams(dimension_semantics=("arbitrary",) * 3, vmem_limit_bytes=VMEM_LIMIT),
    )(a_log, dt_bias, d_skip, mix, proj, proj, proj, proj, dt_t, conv_w, conv_w, conv_w,
      conv_b, conv_b, conv_b, norm_w)


def _inv_unit_lower(a, c):
    ii = _iota2((c, c), 0)
    jj = _iota2((c, c), 1)
    n = -a
    p = jnp.where(ii == jj, 1.0, 0.0) + n
    for _ in range(c.bit_length() - 2):
        n = _dot_hi(n, n)
        p = p + _dot_hi(p, n)
    return p


def _gdn_body(alog_ref, dtb_ref, mix_ref, q_ref, k_ref, v_ref, z_ref, a_ref, b_ref, wq_ref, wk_ref, wv_ref, nw_ref,
              o_ref, s_ref, cq_ref, ck_ref, cv_ref, eq, ek, ev, *, rows, rep, dk, dv):
    del mix_ref
    hk = pl.program_id(1)
    tb = pl.program_id(2)
    last_tb = pl.num_programs(2) - 1
    c = CH
    nch = rows // c

    @pl.when(tb == 0)
    def _():
        s_ref[...] = jnp.zeros_like(s_ref)
        eq[0:8, :] = jnp.zeros((8, eq.shape[1]), F32)
        ek[0:8, :] = jnp.zeros((8, ek.shape[1]), F32)
        ev[0:8, :] = jnp.zeros((8, ev.shape[1]), F32)

    q = _silu(_conv_block(q_ref, eq, wq_ref, rows))
    k = _silu(_conv_block(k_ref, ek, wk_ref, rows))
    v = _silu(_conv_block(v_ref, ev, wv_ref, rows))

    @pl.when(tb == last_tb)
    def _():
        cq_ref[...] = eq[rows + 5:rows + 8, :]
        ck_ref[...] = ek[rows + 5:rows + 8, :]
        cv_ref[...] = ev[rows + 5:rows + 8, :]

    _conv_carry(eq, rows)
    _conv_carry(ek, rows)
    _conv_carry(ev, rows)

    q = q * lax.rsqrt(jnp.sum(q * q, -1, keepdims=True) + 1e-6) * (dk ** -0.5)
    k = k * lax.rsqrt(jnp.sum(k * k, -1, keepdims=True) + 1e-6)

    c2 = 2 * c
    ii = _iota2((c2, c2), 0)
    jj = _iota2((c2, c2), 1)
    same = (ii >= c) == (jj >= c)
    ti = ii & (c - 1)
    tj = jj & (c - 1)
    incl = same & (ti >= tj)
    strict = same & (ti > tj)
    later = same & (tj > ti)
    diag = ii == jj
    top = ii < c
    col2 = _iota2((c2, 2 * c2), 1)
    row2 = _iota2((c2, 2 * c2), 0)
    sel1 = jnp.where(col2 >= c2, 1.0, jnp.where(((row2 >= c) == (col2 >= c)) & ((row2 & (c - 1)) > (col2 & (c - 1))),
                                                 1.0, 0.0)).astype(BF16)
    sel2 = jnp.where((_iota2((2 * c2, 2 * c2), 0) >= c2) == (_iota2((2 * c2, 2 * c2), 1) >= c2), 1.0, 0.0).astype(BF16)
    eye = jnp.where(diag, 1.0, 0.0)
    lane = _iota2((1, c2), 1)
    alog = jnp.where(lane < c, alog_ref[hk * 2], alog_ref[hk * 2 + 1])
    dtb = jnp.where(lane < c, dtb_ref[hk * 2], dtb_ref[hk * 2 + 1])
    neg_a = -jnp.exp(alog)
    nw = nw_ref[...]

    pre = []
    for ch in range(nch):
        sl = slice(ch * c, (ch + 1) * c)
        q2 = jnp.concatenate([q[sl], q[sl]], axis=0)
        k2 = jnp.concatenate([k[sl], k[sl]], axis=0)
        v2 = jnp.concatenate([v[sl, :dv], v[sl, dv:]], axis=0)
        g_row = neg_a * _softplus(a_ref[ch] + dtb)
        beta_row = jax.nn.sigmoid(b_ref[ch])
        out1 = _dot_sel(jnp.where(incl, g_row, 0.0), sel1)
        out2 = _dot_sel(jnp.concatenate([jnp.where(later, g_row, 0.0), jnp.where(diag, beta_row, 0.0)], axis=1), sel2)
        gcum_cb, rest_cb, beta_cb = out1[:, c2:], out2[:, :c2], out2[:, c2:]
        decay = jnp.where(incl, jnp.exp(jnp.where(incl, out1[:, :c2], 0.0)), 0.0)
        eg = jnp.exp(gcum_cb)
        kb2 = k2 * beta_cb
        pre.append(dict(
            n=-jnp.where(strict, _dot_nt(kb2, k2) * decay, 0.0),
            rhs=jnp.concatenate([v2 * beta_cb, kb2 * eg], axis=1),
            qe=q2 * eg,
            qk=_dot_nt(q2, k2) * decay,
            kw=k2 * jnp.exp(rest_cb),
            declast=jnp.concatenate([eg[c - 1:c, :], eg[c2 - 1:c2, :]], axis=1)))

    ps = [eye + d["n"] for d in pre]
    ms = [_dot_hi(d["n"], d["n"]) for d in pre]
    n_steps = c.bit_length() - 2
    for step in range(n_steps):
        if step < n_steps - 1:
            rs = [_dot_hi(jnp.concatenate([p, m], axis=0), m) for p, m in zip(ps, ms)]
            ps = [p + r[:c2] for p, r in zip(ps, rs)]
            ms = [r[c2:] for r in rs]
        else:
            ps = [p + _dot_hi(p, m) for p, m in zip(ps, ms)]
    sols = [_dot_hi(p, d["rhs"]) for p, d in zip(ps, pre)]

    s_cat = jnp.concatenate([s_ref[0], s_ref[1]], axis=1)
    for ch in range(nch):
        sl = slice(ch * c, (ch + 1) * c)
        d = pre[ch]
        sol = sols[ch]
        r = _dot(jnp.concatenate([sol[:, dv:], d["qe"]], axis=0), s_cat)
        u = sol[:, :dv] - jnp.where(top, r[:c2, :dv], r[:c2, dv:])
        o = jnp.where(top, r[c2:, :dv], r[c2:, dv:]) + _dot(d["qk"], u)
        u_bd = jnp.concatenate([jnp.where(top, u, 0.0), jnp.where(top, 0.0, u)], axis=1)
        s_cat = s_cat * d["declast"] + _dot_tn(d["kw"], u_bd)
        zz = jnp.concatenate([z_ref[sl, :dv], z_ref[sl, dv:]], axis=0)
        res = o * lax.rsqrt(jnp.mean(o * o, -1, keepdims=True) + NORM_EPS) * nw * _silu(zz)
        o_ref[sl, :dv] = res[:c]
        o_ref[sl, dv:] = res[c:]
    s_ref[0] = s_cat[:, :dv]
    s_ref[1] = s_cat[:, dv:]


def _gdn_prompt(mix, proj, a_t, b_t, conv_w, a_log, dt_bias, norm_w, *, rows=256):
    hk, hv, dk, dv = GDN_HK, GDN_HV, GDN_DK, GDN_DV
    rep = hv // hk
    assert rep == 2 and dk == dv
    ntb = SEQ // rows
    nch = rows // CH
    vw = rep * dv
    kcol = hk
    vcol = (2 * GDN_QK) // vw
    zcol = (2 * GDN_QK + GDN_VW) // vw
    smem = pl.BlockSpec(memory_space=pltpu.SMEM)
    in_specs = [
        smem, smem,
        pl.BlockSpec(memory_space=pl.ANY),
        pl.BlockSpec((rows, dk), lambda b, h, t: (b * ntb + t, h)),
        pl.BlockSpec((rows, dk), lambda b, h, t: (b * ntb + t, kcol + h)),
        pl.BlockSpec((rows, vw), lambda b, h, t: (b * ntb + t, vcol + h)),
        pl.BlockSpec((rows, vw), lambda b, h, t: (b * ntb + t, zcol + h)),
        pl.BlockSpec((None, nch, 1, rep * CH), lambda b, h, t: (h, b * ntb + t, 0, 0)),
        pl.BlockSpec((None, nch, 1, rep * CH), lambda b, h, t: (h, b * ntb + t, 0, 0)),
        pl.BlockSpec((CONV_W, dk), lambda b, h, t: (0, h)),
        pl.BlockSpec((CONV_W, dk), lambda b, h, t: (0, kcol + h)),
        pl.BlockSpec((CONV_W, vw), lambda b, h, t: (0, vcol + h)),
        pl.BlockSpec((1, dv), lambda b, h, t: (0, 0)),
    ]
    out_specs = [
        pl.BlockSpec((rows, vw), lambda b, h, t: (b * ntb + t, h)),
        pl.BlockSpec((None, rep, dk, dv), lambda b, h, t: (b, h, 0, 0)),
        pl.BlockSpec((None, CONV_W - 1, dk), lambda b, h, t: (b, 0, h)),
        pl.BlockSpec((None, CONV_W - 1, dk), lambda b, h, t: (b, 0, h)),
        pl.BlockSpec((None, CONV_W - 1, vw), lambda b, h, t: (b, 0, h)),
    ]
    out_shape = [
        jax.ShapeDtypeStruct(mix.shape, F32),
        jax.ShapeDtypeStruct((BATCH, hv, dk, dv), F32),
        jax.ShapeDtypeStruct((BATCH, CONV_W - 1, GDN_QK), F32),
        jax.ShapeDtypeStruct((BATCH, CONV_W - 1, GDN_QK), F32),
        jax.ShapeDtypeStruct((BATCH, CONV_W - 1, GDN_VW), F32),
    ]
    scratch = [pltpu.VMEM((rows + 8, dk), F32), pltpu.VMEM((rows + 8, dk), F32), pltpu.VMEM((rows + 8, vw), F32)]
    return pl.pallas_call(
        functools.partial(_gdn_body, rows=rows, rep=rep, dk=dk, dv=dv),
        grid=(BATCH, hk, ntb),
        in_specs=in_specs, out_specs=out_specs, out_shape=out_shape, scratch_shapes=scratch,
        input_output_aliases={2: 0},
        compiler_params=pltpu.CompilerParams(dimension_semantics=("arbitrary",) * 3, vmem_limit_bytes=VMEM_LIMIT),
    )(a_log, dt_bias, mix, proj, proj, proj, proj, a_t, b_t, conv_w, conv_w, conv_w, norm_w)


def _layer_norm(xf, g, b):
    xc = xf - jnp.mean(xf, -1, keepdims=True)
    var = jnp.mean(xc * xc, -1, keepdims=True)
    return xc * lax.rsqrt(var + NORM_EPS) * g + b


def _rms(xf):
    return xf * lax.rsqrt(jnp.mean(xf * xf, -1, keepdims=True) + NORM_EPS)


def _head_ln(xf):
    xc = xf - jnp.mean(xf, -1, keepdims=True)
    return xc * lax.rsqrt(jnp.mean(xc * xc, -1, keepdims=True) + NORM_EPS)


def _l2norm(xf):
    return xf * lax.rsqrt(jnp.sum(xf * xf, -1, keepdims=True) + 1e-6)


def _rotary(t, pos):
    half = t.shape[-1] // 2
    inv = ROPE_BASE ** (-jnp.arange(half, dtype=F32) / half)
    ang = pos.astype(F32)[:, None] * inv
    cos = jnp.cos(ang)[None, :, None, :]
    sin = jnp.sin(ang)[None, :, None, :]
    t1, t2 = t[..., :half], t[..., half:]
    return jnp.concatenate([t1 * cos - t2 * sin, t1 * sin + t2 * cos], -1)


def _chunk_len(L):
    return CHUNK if L % CHUNK == 0 else L


def _to_chunks(t, c):
    return t.reshape(t.shape[0], t.shape[1] // c, c, *t.shape[2:]).swapaxes(0, 1)


def _from_chunks(t):
    t = t.swapaxes(0, 1)
    return t.reshape(t.shape[0], t.shape[1] * t.shape[2], *t.shape[3:])


def _causal_conv(x, prev, w, b):
    L = x.shape[1]
    xp = jnp.concatenate([prev.astype(x.dtype), x], axis=1)
    y = xp[:, CONV_W - 1:] * w[CONV_W - 1]
    for i in range(CONV_W - 1):
        y = y + xp[:, i:i + L] * w[i]
    if b is not None:
        y = y + b
    return y, xp[:, L:]


def _retention_scan(q, k, v, s0, log_gamma):
    L = q.shape[1]
    c = _chunk_len(L)
    idx = jnp.arange(c, dtype=F32)
    diff = idx[:, None] - idx[None, :]
    dmask = jnp.exp(jnp.where((diff >= 0)[None], log_gamma[:, None, None] * diff[None], -jnp.inf))
    q_dec = jnp.exp((idx[:, None] + 1.0) * log_gamma)[:, :, None]
    k_dec = jnp.exp((c - 1.0 - idx)[:, None] * log_gamma)[:, :, None]
    c_dec = jnp.exp(c * log_gamma)[:, None, None]

    def step(s, inp):
        qc, kc, vc = inp
        att = jnp.einsum('bihd,bjhd->bhij', qc, kc) * dmask
        o = jnp.einsum('bhij,bjhe->bihe', att, vc) + jnp.einsum('bihd,bhde->bihe', qc * q_dec, s)
        s = s * c_dec + jnp.einsum('bjhd,bjhe->bhde', kc * k_dec, vc)
        return s, o

    s, o = lax.scan(step, s0, (_to_chunks(q, c), _to_chunks(k, c), _to_chunks(v, c)))
    return _from_chunks(o), s


def _ssd_scan(x, dt, a, bm, cm, s0):
    L = x.shape[1]
    c = _chunk_len(L)
    ar = jnp.arange(c)
    mask5 = (ar[:, None] >= ar[None, :])[None, :, :, None, None]

    def step(s, inp):
        xc, dtc, bc, cc = inp
        cs = jnp.cumsum(dtc * a, axis=1)
        lmat = jnp.exp(jnp.where(mask5, cs[:, :, None] - cs[:, None, :], -jnp.inf))
        cb = jnp.einsum('bign,bjgn->bijg', cc, bc)
        y = jnp.einsum('bijg,bijgr,bjgr,bjgrp->bigrp', cb, lmat, dtc, xc)
        y = y + jnp.einsum('bign,bgrpn->bigrp', cc, s) * jnp.exp(cs)[..., None]
        last = cs[:, -1]
        w_end = jnp.exp(last[:, None] - cs) * dtc
        s = s * jnp.exp(last)[..., None, None] + jnp.einsum('bjgr,bjgn,bjgrp->bgrpn', w_end, bc, xc)
        return s, y

    s, y = lax.scan(step, s0, (_to_chunks(x, c), _to_chunks(dt, c), _to_chunks(bm, c), _to_chunks(cm, c)))
    return _from_chunks(y), s


def _gdn_scan(q, k, v, g, beta, s0):
    L = q.shape[1]
    dv = v.shape[-1]
    c = _chunk_len(L)
    ar = jnp.arange(c)
    incl = ar[:, None] >= ar[None, :]
    strict = ar[:, None] > ar[None, :]
    eye = jnp.eye(c, dtype=F32)

    def step(s, inp):
        qc, kc, vc, gc, bc = inp
        gcum = jnp.cumsum(gc, axis=1)
        gh = gcum.swapaxes(1, 2)
        decay = jnp.exp(jnp.where(incl, gh[..., :, None] - gh[..., None, :], -jnp.inf))
        kb = kc * bc[..., None]
        a_mat = jnp.where(strict, jnp.einsum('bihd,bjhd->bhij', kb, kc) * decay, 0.0)
        rhs = jnp.concatenate([(vc * bc[..., None]).swapaxes(1, 2),
                               (kb * jnp.exp(gcum)[..., None]).swapaxes(1, 2)], -1)
        sol = lax.linalg.triangular_solve(a_mat + eye, rhs, left_side=True, lower=True, unit_diagonal=True)
        u = sol[..., :dv] - jnp.einsum('bhid,bhde->bhie', sol[..., dv:], s)
        qk = jnp.einsum('bihd,bjhd->bhij', qc, kc) * decay
        o = jnp.einsum('bihd,bhde->bihe', qc * jnp.exp(gcum)[..., None], s) + jnp.einsum('bhij,bhje->bihe', qk, u)
        g_last = gcum[:, -1]
        s = s * jnp.exp(g_last)[..., None, None] + jnp.einsum(
            'bjhd,bhje->bhde', kc * jnp.exp(g_last[:, None] - gcum)[..., None], u)
        return s, o

    xs = (_to_chunks(q, c), _to_chunks(k, c), _to_chunks(v, c), _to_chunks(g, c), _to_chunks(beta, c))
    s, o = lax.scan(step, s0, xs)
    return _from_chunks(o), s


def _even_core(proj, dt_raw, pos, s_ret, s_ssm, conv_prev, conv_w, conv_b, dt_bias, a_log, d_skip, norm_w):
    bsz, L, _ = proj.shape
    q, k, v, g, z, xbc = _split(proj, EVEN_SPLITS[:-1])
    q = _rotary(q.reshape(bsz, L, RET_HEADS, RET_DK), pos)
    k = _rotary(k.reshape(bsz, L, RET_HEADS, RET_DK), pos) * RET_DK ** -0.5
    v = v.reshape(bsz, L, RET_HEADS, RET_DV)
    log_gamma = jnp.log1p(-jnp.exp2(-5.0 - jnp.arange(RET_HEADS, dtype=F32)))
    o_ret, s_ret_new = _retention_scan(q, k, v, s_ret, log_gamma)
    o_ret = jax.nn.silu(g) * _head_ln(o_ret).reshape(bsz, L, -1)
    xbc, conv_new = _causal_conv(xbc, conv_prev, conv_w, conv_b)
    xs, bm, cm = _split(jax.nn.silu(xbc), (SSM_INNER, SSM_GROUPS * SSM_N, SSM_GROUPS * SSM_N))
    r = SSM_HEADS // SSM_GROUPS
    xs = xs.reshape(bsz, L, SSM_GROUPS, r, SSM_P)
    dt = jax.nn.softplus(dt_raw + dt_bias).reshape(bsz, L, SSM_GROUPS, r)
    a = -jnp.exp(a_log).reshape(SSM_GROUPS, r)
    y, s_ssm_new = _ssd_scan(xs, dt, a, bm.reshape(bsz, L, SSM_GROUPS, SSM_N), cm.reshape(bsz, L, SSM_GROUPS, SSM_N),
                             s_ssm.reshape(bsz, SSM_GROUPS, r, SSM_P, SSM_N))
    y = y + d_skip.reshape(SSM_GROUPS, r, 1) * xs
    y = y.reshape(bsz, L, SSM_INNER) * jax.nn.silu(z)
    y = _rms(y.reshape(bsz, L, SSM_GROUPS, -1)).reshape(bsz, L, SSM_INNER) * norm_w
    mix = jnp.concatenate([o_ret, y], -1)
    return mix, s_ret_new, s_ssm_new.reshape(bsz, SSM_HEADS, SSM_P, SSM_N), conv_new


def _odd_core(proj, ab, s_gdn, conv_prev, conv_w, dt_bias, a_log, norm_w):
    bsz, L, _ = proj.shape
    qkv, z = _split(proj, ODD_SPLITS[:2])
    a, b = _split(ab, ODD_SPLITS[2:])
    qkv, conv_new = _causal_conv(qkv, conv_prev, conv_w, None)
    q, k, v = _split(jax.nn.silu(qkv), (GDN_QK, GDN_QK, GDN_VW))
    rep = GDN_HV // GDN_HK
    q = jnp.repeat(_l2norm(q.reshape(bsz, L, GDN_HK, GDN_DK)), rep, axis=2) * GDN_DK ** -0.5
    k = jnp.repeat(_l2norm(k.reshape(bsz, L, GDN_HK, GDN_DK)), rep, axis=2)
    v = v.reshape(bsz, L, GDN_HV, GDN_DV)
    g = -jnp.exp(a_log) * jax.nn.softplus(a + dt_bias)
    beta = jax.nn.sigmoid(b)
    o, s_new = _gdn_scan(q, k, v, g, beta, s_gdn)
    o = _rms(o) * norm_w * jax.nn.silu(z.reshape(bsz, L, GDN_HV, GDN_DV))
    return o.reshape(bsz, L, GDN_VW), s_new, conv_new


def _post_block(x, mix_out, p, layer, ln1_g, ln1_b, ln2_g, ln2_b, router_w, router_b, we_gate, we_up, we_down,
                ws_gate, ws_up, ws_down, ple_proj, ple_gate):
    n_tok = x.shape[0]
    x, xb, idx, wts = _ln_router(x, mix_out, ln1_g, ln1_b, router_w, router_b, layer, tm=520)
    shared = _shared_expert(xb, ws_gate, ws_up, ws_down, layer, tm=520)
    rows_tok, rows_gate, pair_row, be, n_used = _route_plan(idx, wts, n_tok)
    x_pad = jnp.concatenate([xb, jnp.zeros((1, D_MODEL), BF16)], 0)
    y_rows = _experts(x_pad[rows_tok], rows_gate, be, n_used, we_gate, we_up, we_down, layer)
    y_k = y_rows[pair_row.T.reshape(-1)].reshape(TOP_K, n_tok, D_MODEL)
    x = _combine_ln(y_k, shared, x, ln2_g, ln2_b, layer, tm=104)
    return _ple(x, p, ple_gate, ple_proj, layer, tm=1040, tn=512)


def kernel(x_prompt, x_sample, state_ret, state_ssm, state_ssm_conv, state_gdn, state_gdn_conv, p_prompt, p_sample, ev_w_in, ev_w_out, ssm_conv_w, ssm_conv_b, ssm_dt_bias, ssm_a_log, ssm_d, ssm_norm_w, od_w_in, od_w_out, gdn_conv_w, gdn_dt_bias, gdn_a_log, gdn_norm_w, ln1_g, ln1_b, ln2_g, ln2_b, router_w, router_b, exp_w_gate, exp_w_up, exp_w_down, sh_w_gate, sh_w_up, sh_w_down, ple_proj, ple_gate):
    bp, lp = x_prompt.shape[0], x_prompt.shape[1]
    ls = x_sample.shape[1]
    pos_p = jnp.arange(lp, dtype=jnp.int32)
    pos_s = PAST_LEN + jnp.arange(ls, dtype=jnp.int32)
    x = jnp.concatenate([x_prompt.reshape(N_PROMPT, D_MODEL), x_sample.reshape(DEC_BATCH, D_MODEL)], 0)
    p_all = jnp.concatenate([p_prompt.reshape(DEPTH, N_PROMPT, D_PLE), p_sample.reshape(DEPTH, DEC_BATCH, D_PLE)], 1)
    ret_p, ret_s, ssm_p, ssm_s, sconv_p, sconv_s = [], [], [], [], [], []
    gdn_p, gdn_s, gconv_p, gconv_s = [], [], [], []
    half = RET_DK // 2
    ang = pos_p.astype(F32)[:, None] * (ROPE_BASE ** (-jnp.arange(half, dtype=F32) / half))
    cos_p, sin_p = jnp.cos(ang), jnp.sin(ang)
    log_gamma = jnp.log1p(-jnp.exp2(-5.0 - jnp.arange(RET_HEADS, dtype=F32)))
    for i in range(DEPTH):
        j = i // 2
        if i % 2 == 0:
            proj = _dense(x, ev_w_in, (j,), n_cols=EVEN_MAIN, tm=1040, tn=512)
            dt_raw = _dense(x, ev_w_in[j, :, EVEN_MAIN:], tm=1040, tn=SSM_HEADS)
            w = (ssm_conv_w[j], ssm_conv_b[j], ssm_dt_bias[j], ssm_a_log[j], ssm_d[j], ssm_norm_w[j])
            mix, r1 = _retention_prompt(jnp.zeros((N_TOK, EVEN_MIX), F32), proj, cos_p, sin_p, log_gamma)
            dt_t = dt_raw[:N_PROMPT].T.reshape(SSM_HEADS, N_PROMPT // CH, 1, CH)
            mix, r2, cx, cb, cc = _ssd_prompt(mix, proj, dt_t, ssm_conv_w[j], ssm_conv_b[j].reshape(1, -1),
                                              ssm_a_log[j], ssm_dt_bias[j], ssm_d[j], ssm_norm_w[j].reshape(1, -1))
            r2 = r2.reshape(bp, SSM_HEADS, SSM_P, SSM_N)
            r3 = jnp.concatenate([cx, cb, cc], -1)
            ms, q1, q2, q3 = _even_core(proj[N_PROMPT:].reshape(DEC_BATCH, ls, -1),
                                        dt_raw[N_PROMPT:].reshape(DEC_BATCH, ls, -1),
                                        pos_s, state_ret[j], state_ssm[j], state_ssm_conv[j], *w)
            ret_p.append(r1); ssm_p.append(r2); sconv_p.append(r3)
            ret_s.append(q1); ssm_s.append(q2); sconv_s.append(q3)
            mix = lax.dynamic_update_slice(mix, ms.reshape(DEC_BATCH, -1), (N_PROMPT, 0))
            mix_out = _dense(mix, ev_w_out, (j,), tm=520, tn=512)
        else:
            proj = _dense(x, od_w_in, (j,), n_cols=ODD_MAIN, tm=1040, tn=512)
            ab = _dense(x, od_w_in[j, :, ODD_MAIN:], tm=1040, tn=2 * GDN_HV)
            w = (gdn_conv_w[j], gdn_dt_bias[j], gdn_a_log[j], gdn_norm_w[j])
            ab_t = ab[:N_PROMPT].reshape(N_PROMPT // CH, CH, 2, GDN_HK, 2).transpose(2, 3, 0, 4, 1)
            ab_t = ab_t.reshape(2, GDN_HK, N_PROMPT // CH, 1, 2 * CH)
            mix, r1, cq, ck, cv = _gdn_prompt(jnp.zeros((N_TOK, GDN_VW), F32), proj, ab_t[0], ab_t[1], gdn_conv_w[j],
                                              gdn_a_log[j], gdn_dt_bias[j], gdn_norm_w[j].reshape(1, -1))
            r2 = jnp.concatenate([cq, ck, cv], -1)
            ms, q1, q2 = _odd_core(proj[N_PROMPT:].reshape(DEC_BATCH, ls, -1), ab[N_PROMPT:].reshape(DEC_BATCH, ls, -1),
                                   state_gdn[j], state_gdn_conv[j], *w)
            gdn_p.append(r1); gconv_p.append(r2)
            gdn_s.append(q1); gconv_s.append(q2)
            mix = lax.dynamic_update_slice(mix, ms.reshape(DEC_BATCH, -1), (N_PROMPT, 0))
            mix_out = _dense(mix, od_w_out, (j,), tm=520, tn=512)
        x = _post_block(x, mix_out, p_all, i, ln1_g, ln1_b, ln2_g, ln2_b, router_w, router_b,
                        exp_w_gate, exp_w_up, exp_w_down, sh_w_gate, sh_w_up, sh_w_down, ple_proj, ple_gate)
    xp = x[:N_PROMPT].reshape(bp, lp, D_MODEL)
    xs = x[N_PROMPT:].reshape(DEC_BATCH, ls, D_MODEL)
    return (xp, xs, jnp.stack(ret_p), jnp.stack(ret_s), jnp.stack(ssm_p), jnp.stack(ssm_s),
            jnp.stack(sconv_p), jnp.stack(sconv_s), jnp.stack(gdn_p), jnp.stack(gdn_s),
            jnp.stack(gconv_p), jnp.stack(gconv_s))
```

```python
import functools
import math

import jax
import jax.numpy as jnp
from jax import lax
from jax.experimental import pallas as pl
from jax.experimental.pallas import tpu as pltpu

F32 = jnp.float32
BF16 = jnp.bfloat16

D_MODEL = 2048
BATCH = 4
SEQ = 2048
DEPTH = 4
DEC_BATCH = 128
PAST_LEN = 16384
CHUNK = 64
CONV_W = 4
NORM_EPS = 1e-5
RET_HEADS = 8
RET_DK = D_MODEL // RET_HEADS
RET_DV = D_MODEL // RET_HEADS
ROPE_BASE = 10000.0
SSM_HEADS = 32
SSM_P = D_MODEL // SSM_HEADS
SSM_INNER = SSM_HEADS * SSM_P
SSM_GROUPS = 4
SSM_N = 128
SSM_CONV_CH = SSM_INNER + 2 * SSM_GROUPS * SSM_N
GDN_HK = 16
GDN_HV = 32
GDN_DK = 128
GDN_DV = 128
GDN_QK = GDN_HK * GDN_DK
GDN_VW = GDN_HV * GDN_DV
GDN_CONV_CH = 2 * GDN_QK + GDN_VW
EVEN_SPLITS = (RET_HEADS * RET_DK, RET_HEADS * RET_DK, RET_HEADS * RET_DV, RET_HEADS * RET_DV,
               SSM_INNER, SSM_CONV_CH, SSM_HEADS)
EVEN_MAIN = sum(EVEN_SPLITS[:-1])
EVEN_MIX = RET_HEADS * RET_DV + SSM_INNER
ODD_SPLITS = (GDN_CONV_CH, GDN_VW, GDN_HV, GDN_HV)
ODD_MAIN = GDN_CONV_CH + GDN_VW
N_EXPERTS = 64
TOP_K = 8
D_EXPERT = D_MODEL // 4
ROUTED_SCALE = 2.5
D_PLE = 256
ALPHA = (2 * DEPTH) ** 0.25

N_PROMPT = BATCH * SEQ
N_TOK = N_PROMPT + DEC_BATCH
EXPERT_ROWS = 256
VMEM_LIMIT = 56 * 1024 * 1024


def _split(t, sizes):
    out, start = [], 0
    for s in sizes:
        out.append(t[..., start:start + s])
        start += s
    return out


def _dense_body(x_ref, w_ref, o_ref, wb_ref):
    @pl.when(pl.program_id(1) == 0)
    def _():
        wb_ref[...] = w_ref[...].astype(BF16)

    o_ref[...] = jnp.dot(x_ref[...].astype(BF16), wb_ref[...], preferred_element_type=F32)


def _dense(x, w, lead=(), *, n_cols=None, tm, tn):
    m, k = x.shape
    n_cols = w.shape[-1] if n_cols is None else n_cols
    assert m % tm == 0 and n_cols % tn == 0
    nl = len(lead)
    w_block = (None,) * nl + (k, tn)
    return pl.pallas_call(
        _dense_body,
        grid=(n_cols // tn, m // tm),
        in_specs=[pl.BlockSpec((tm, k), lambda j, i: (i, 0)),
                  pl.BlockSpec(w_block, lambda j, i: lead + (0, j))],
        out_specs=pl.BlockSpec((tm, tn), lambda j, i: (i, j)),
        out_shape=jax.ShapeDtypeStruct((m, n_cols), F32),
        scratch_shapes=[pltpu.VMEM((k, tn), BF16)],
        compiler_params=pltpu.CompilerParams(dimension_semantics=("arbitrary", "arbitrary"),
                                             vmem_limit_bytes=VMEM_LIMIT),
    )(x, w)


def _dense_f32_body(x_ref, w_ref, o_ref):
    o_ref[...] = jnp.dot(x_ref[...].astype(BF16), w_ref[...].astype(BF16), preferred_element_type=F32)


def _dense_f32(x, w, lead=(), *, tm):
    m, k = x.shape
    n = w.shape[-1]
    nl = len(lead)
    return pl.pallas_call(
        _dense_f32_body,
        grid=(m // tm,),
        in_specs=[pl.BlockSpec((tm, k), lambda i: (i, 0)),
                  pl.BlockSpec((None,) * nl + (k, n), lambda i: lead + (0, 0))],
        out_specs=pl.BlockSpec((tm, n), lambda i: (i, 0)),
        out_shape=jax.ShapeDtypeStruct((m, n), F32),
        compiler_params=pltpu.CompilerParams(dimension_semantics=("arbitrary",),
                                             vmem_limit_bytes=VMEM_LIMIT),
    )(x, w)


def _ln_router_body(x_ref, m_ref, g_ref, b_ref, rw_ref, rb_ref, x1_ref, xb_ref, idx_ref, wts_ref):
    acc = ALPHA * x_ref[...] + m_ref[...]
    xc = acc - jnp.mean(acc, -1, keepdims=True)
    var = jnp.mean(xc * xc, -1, keepdims=True)
    x1 = xc * lax.rsqrt(var + NORM_EPS) * g_ref[...] + b_ref[...]
    x1_ref[...] = x1
    xb = x1.astype(BF16)
    xb_ref[...] = xb
    scores = jax.nn.sigmoid(jnp.dot(xb, rw_ref[...].astype(BF16), preferred_element_type=F32))
    sel = scores + rb_ref[...]
    tm, ne = sel.shape
    lane = _iota2((tm, ne), 1).astype(F32)
    slot = _iota2((tm, TOP_K), 1)
    idx = jnp.zeros((tm, TOP_K), F32)
    wts = jnp.zeros((tm, TOP_K), F32)
    for r in range(TOP_K):
        best = jnp.max(sel, axis=-1, keepdims=True)
        arg = jnp.min(jnp.where(sel == best, lane, float(ne)), axis=-1, keepdims=True)
        pick = lane == arg
        idx = jnp.where(slot == r, arg, idx)
        wts = jnp.where(slot == r, jnp.sum(jnp.where(pick, scores, 0.0), axis=-1, keepdims=True), wts)
        sel = jnp.where(pick, -jnp.inf, sel)
    idx_ref[...] = idx.astype(jnp.int32)
    wts_ref[...] = wts / jnp.sum(wts, -1, keepdims=True) * ROUTED_SCALE


def _ln_router(x, mix_out, ln_g, ln_b, router_w, router_b, layer, *, tm):
    n_tok, d = x.shape
    row = lambda w: pl.BlockSpec((tm, w), lambda i: (i, 0))
    par = lambda *shape: pl.BlockSpec((None,) + shape, lambda i: (layer,) + (0,) * len(shape))
    return pl.pallas_call(
        _ln_router_body,
        grid=(n_tok // tm,),
        in_specs=[row(d), row(d), par(1, d), par(1, d), par(d, N_EXPERTS), par(1, N_EXPERTS)],
        out_specs=[row(d), row(d), row(TOP_K), row(TOP_K)],
        out_shape=[jax.ShapeDtypeStruct((n_tok, d), F32), jax.ShapeDtypeStruct((n_tok, d), BF16),
                   jax.ShapeDtypeStruct((n_tok, TOP_K), jnp.int32), jax.ShapeDtypeStruct((n_tok, TOP_K), F32)],
        compiler_params=pltpu.CompilerParams(dimension_semantics=("arbitrary",), vmem_limit_bytes=VMEM_LIMIT),
    )(x, mix_out, ln_g.reshape(DEPTH, 1, d), ln_b.reshape(DEPTH, 1, d), router_w, router_b.reshape(DEPTH, 1, N_EXPERTS))


def _shared_body(x_ref, wg_ref, wu_ref, wd_ref, o_ref, wgb, wub, wdb):
    @pl.when(pl.program_id(0) == 0)
    def _():
        wgb[...] = wg_ref[...].astype(BF16)
        wub[...] = wu_ref[...].astype(BF16)
        wdb[...] = wd_ref[...].astype(BF16)

    xb = x_ref[...]
    g = jnp.dot(xb, wgb[...], preferred_element_type=F32)
    u = jnp.dot(xb, wub[...], preferred_element_type=F32)
    h = (g * jax.nn.sigmoid(g)) * u
    o_ref[...] = jnp.dot(h.astype(BF16), wdb[...], preferred_element_type=F32)


def _shared_expert(xb, ws_gate, ws_up, ws_down, layer, *, tm):
    n_tok, d = xb.shape
    ds = ws_gate.shape[-1]
    return pl.pallas_call(
        _shared_body,
        grid=(n_tok // tm,),
        in_specs=[pl.BlockSpec((tm, d), lambda i: (i, 0)),
                  pl.BlockSpec((None, d, ds), lambda i: (layer, 0, 0)),
                  pl.BlockSpec((None, d, ds), lambda i: (layer, 0, 0)),
                  pl.BlockSpec((None, ds, d), lambda i: (layer, 0, 0))],
        out_specs=pl.BlockSpec((tm, d), lambda i: (i, 0)),
        out_shape=jax.ShapeDtypeStruct((n_tok, d), F32),
        scratch_shapes=[pltpu.VMEM((d, ds), BF16), pltpu.VMEM((d, ds), BF16), pltpu.VMEM((ds, d), BF16)],
        compiler_params=pltpu.CompilerParams(dimension_semantics=("arbitrary",), vmem_limit_bytes=VMEM_LIMIT),
    )(xb, ws_gate, ws_up, ws_down)


def _ple_body(x_ref, xc_ref, p_ref, wg_ref, wp_ref, o_ref, wgb, wpb):
    @pl.when(pl.program_id(1) == 0)
    def _():
        wgb[...] = wg_ref[...].astype(BF16)
        wpb[...] = wp_ref[...].astype(BF16)

    gate = jnp.dot(x_ref[...].astype(BF16), wgb[...], preferred_element_type=F32)
    emb = jnp.dot(p_ref[...].astype(BF16), wpb[...], preferred_element_type=F32)
    o_ref[...] = xc_ref[...] + jax.nn.sigmoid(gate) * emb


def _ple(x, p, ple_gate, ple_proj, layer, *, tm, tn):
    n_tok, d = x.shape
    dp = p.shape[-1]
    return pl.pallas_call(
        _ple_body,
        grid=(d // tn, n_tok // tm),
        in_specs=[pl.BlockSpec((tm, d), lambda j, i: (i, 0)),
                  pl.BlockSpec((tm, tn), lambda j, i: (i, j)),
                  pl.BlockSpec((None, tm, dp), lambda j, i: (layer, i, 0)),
                  pl.BlockSpec((None, d, tn), lambda j, i: (layer, 0, j)),
                  pl.BlockSpec((None, dp, tn), lambda j, i: (layer, 0, j))],
        out_specs=pl.BlockSpec((tm, tn), lambda j, i: (i, j)),
        out_shape=jax.ShapeDtypeStruct((n_tok, d), F32),
        scratch_shapes=[pltpu.VMEM((d, tn), BF16), pltpu.VMEM((dp, tn), BF16)],
        compiler_params=pltpu.CompilerParams(dimension_semantics=("arbitrary", "arbitrary"),
                                             vmem_limit_bytes=VMEM_LIMIT),
    )(x, x, p, ple_gate, ple_proj)


def _experts_body(be_ref, nu_ref, x_ref, gate_ref, wg_ref, wu_ref, wd_ref, y_ref, wgb, wub, wdb):
    b = pl.program_id(0)
    e = be_ref[b]
    prev = be_ref[jnp.maximum(b - 1, 0)]

    @pl.when((b == 0) | (e != prev))
    def _():
        wgb[...] = wg_ref[...].astype(BF16)
        wub[...] = wu_ref[...].astype(BF16)
        wdb[...] = wd_ref[...].astype(BF16)

    @pl.when(b < nu_ref[0])
    def _():
        xb = x_ref[...]
        g = jnp.dot(xb, wgb[...], preferred_element_type=F32)
        u = jnp.dot(xb, wub[...], preferred_element_type=F32)
        h = (g * jax.nn.sigmoid(g)) * u
        y_ref[...] = jnp.dot(h.astype(BF16), wdb[...], preferred_element_type=F32) * gate_ref[...]

    @pl.when(b >= nu_ref[0])
    def _():
        y_ref[...] = jnp.zeros_like(y_ref)


def _experts(xg, row_gate, block_expert, n_used, we_gate, we_up, we_down, layer):
    rows, d = xg.shape
    tm = EXPERT_ROWS
    n_blocks = rows // tm
    grid_spec = pltpu.PrefetchScalarGridSpec(
        num_scalar_prefetch=2,
        grid=(n_blocks,),
        in_specs=[pl.BlockSpec((tm, d), lambda b, be, nu: (b, 0)),
                  pl.BlockSpec((tm, 1), lambda b, be, nu: (b, 0)),
                  pl.BlockSpec((None, None, d, D_EXPERT), lambda b, be, nu: (layer, be[b], 0, 0)),
                  pl.BlockSpec((None, None, d, D_EXPERT), lambda b, be, nu: (layer, be[b], 0, 0)),
                  pl.BlockSpec((None, None, D_EXPERT, d), lambda b, be, nu: (layer, be[b], 0, 0))],
        out_specs=pl.BlockSpec((tm, d), lambda b, be, nu: (b, 0)),
        scratch_shapes=[pltpu.VMEM((d, D_EXPERT), BF16), pltpu.VMEM((d, D_EXPERT), BF16),
                        pltpu.VMEM((D_EXPERT, d), BF16)],
    )
    return pl.pallas_call(
        _experts_body,
        grid_spec=grid_spec,
        out_shape=jax.ShapeDtypeStruct((rows, d), F32),
        compiler_params=pltpu.CompilerParams(dimension_semantics=("arbitrary",),
                                             vmem_limit_bytes=VMEM_LIMIT),
    )(block_expert, n_used, xg, row_gate, we_gate, we_up, we_down)


def _route_plan(idx, wts, n_tok):
    tm = EXPERT_ROWS
    n_pairs = n_tok * TOP_K
    n_blocks = (n_pairs + N_EXPERTS * (tm - 1) + tm - 1) // tm
    flat_e = idx.reshape(-1).astype(jnp.int32)
    order = jnp.argsort(flat_e, stable=True).astype(jnp.int32)
    sorted_e = flat_e[order]
    grp_end = jnp.searchsorted(sorted_e, jnp.arange(N_EXPERTS, dtype=jnp.int32), side='right').astype(jnp.int32)
    counts = grp_end - jnp.concatenate([jnp.zeros((1,), jnp.int32), grp_end[:-1]])
    padded = (counts + tm - 1) // tm * tm
    pad_end = jnp.cumsum(padded)
    pad_start = pad_end - padded
    grp_start = jnp.cumsum(counts) - counts
    dest = pad_start[sorted_e] + jnp.arange(n_pairs, dtype=jnp.int32) - grp_start[sorted_e]
    pair_row = jnp.zeros((n_pairs,), jnp.int32).at[order].set(dest, unique_indices=True)
    n_used = (pad_end[-1] // tm).astype(jnp.int32)
    blk = jnp.arange(n_blocks, dtype=jnp.int32)
    be = jnp.minimum(jnp.searchsorted(pad_end, blk * tm, side='right'), N_EXPERTS - 1).astype(jnp.int32)
    row = jnp.arange(n_blocks * tm, dtype=jnp.int32)
    row_e = be[row // tm]
    off = row - pad_start[row_e]
    valid = (off < counts[row_e]) & (row < pad_end[-1])
    pair = order[jnp.clip(grp_start[row_e] + off, 0, n_pairs - 1)]
    rows_tok = jnp.where(valid, pair // TOP_K, n_tok)
    rows_gate = jnp.where(valid, wts.reshape(-1)[pair], 0.0)
    be = jnp.where(blk < n_used, be, be[jnp.maximum(n_used - 1, 0)])
    return rows_tok, rows_gate.reshape(-1, 1), pair_row.reshape(n_tok, TOP_K), be, n_used.reshape(1)


def _combine_body(cur_ref, nxt_ref, y_hbm, sh_ref, x_ref, g_ref, b_ref, o_ref, buf, sem, *, tm):
    i = pl.program_id(0)
    n = pl.num_programs(0)
    slot = i % 2
    n_rows = TOP_K * tm

    def row_copy(src_row, dst_slot, r):
        return pltpu.make_async_copy(y_hbm.at[pl.ds(src_row, 1), :], buf.at[dst_slot, pl.ds(r, 1), :],
                                     sem.at[dst_slot])

    def issue(idx_ref, dst_slot):
        def body(r, carry):
            row_copy(idx_ref[0, r], dst_slot, r).start()
            return carry
        lax.fori_loop(0, n_rows, body, 0, unroll=8)

    @pl.when(i == 0)
    def _():
        issue(cur_ref, 0)

    @pl.when(i + 1 < n)
    def _():
        issue(nxt_ref, 1 - slot)

    def wait_body(r, carry):
        row_copy(0, slot, r).wait()
        return carry
    lax.fori_loop(0, n_rows, wait_body, 0, unroll=8)

    acc = ALPHA * x_ref[...] + sh_ref[...]
    for kk in range(TOP_K):
        acc = acc + buf[slot, kk * tm:(kk + 1) * tm, :]
    xc = acc - jnp.mean(acc, -1, keepdims=True)
    var = jnp.mean(xc * xc, -1, keepdims=True)
    o_ref[...] = xc * lax.rsqrt(var + NORM_EPS) * g_ref[...] + b_ref[...]


def _combine_ln(y_rows, pair_row, shared, x, ln_g, ln_b, layer, *, tm):
    n_tok, d = x.shape
    n_steps = n_tok // tm
    idx = pair_row.reshape(n_steps, tm, TOP_K).transpose(0, 2, 1).reshape(n_steps, 1, TOP_K * tm)
    idx_spec = lambda f: pl.BlockSpec((None, 1, TOP_K * tm), f, memory_space=pltpu.SMEM)
    return pl.pallas_call(
        functools.partial(_combine_body, tm=tm),
        grid=(n_steps,),
        in_specs=[idx_spec(lambda i: (i, 0, 0)),
                  idx_spec(lambda i: (jnp.minimum(i + 1, n_steps - 1), 0, 0)),
                  pl.BlockSpec(memory_space=pl.ANY),
                  pl.BlockSpec((tm, d), lambda i: (i, 0)),
                  pl.BlockSpec((tm, d), lambda i: (i, 0)),
                  pl.BlockSpec((None, 1, d), lambda i: (layer, 0, 0)),
                  pl.BlockSpec((None, 1, d), lambda i: (layer, 0, 0))],
        out_specs=pl.BlockSpec((tm, d), lambda i: (i, 0)),
        out_shape=jax.ShapeDtypeStruct((n_tok, d), F32),
        scratch_shapes=[pltpu.VMEM((2, TOP_K * tm, d), F32), pltpu.SemaphoreType.DMA((2,))],
        compiler_params=pltpu.CompilerParams(dimension_semantics=("arbitrary",), vmem_limit_bytes=VMEM_LIMIT),
    )(idx, idx, y_rows, shared, x, ln_g.reshape(DEPTH, 1, d), ln_b.reshape(DEPTH, 1, d))


CH = 64


def _dot(a, b):
    return jnp.dot(a.astype(BF16), b.astype(BF16), preferred_element_type=F32)


def _dot_nt(a, b):
    return lax.dot_general(a.astype(BF16), b.astype(BF16), (((1,), (1,)), ((), ())), preferred_element_type=F32)


def _dot_tn(a, b):
    return lax.dot_general(a.astype(BF16), b.astype(BF16), (((0,), (0,)), ((), ())), preferred_element_type=F32)


def _split2(a):
    hi = a.astype(BF16)
    lo = (a - hi.astype(F32)).astype(BF16)
    return hi, lo


def _dot_sel(a, sel):
    hi = a.astype(BF16)
    r = a - hi.astype(F32)
    mid = r.astype(BF16)
    lo = (r - mid.astype(F32)).astype(BF16)
    d = lambda x: jnp.dot(x, sel, preferred_element_type=F32)
    return d(hi) + d(mid) + d(lo)


def _dot_hi(a, b):
    ah, al = _split2(a)
    bh, bl = _split2(b)
    d = lambda x, y: jnp.dot(x, y, preferred_element_type=F32)
    return d(ah, bh) + d(ah, bl) + d(al, bh)


def _iota2(shape, dim):
    return lax.broadcasted_iota(jnp.int32, shape, dim)


def _decay_terms(g_row, c):
    ii = _iota2((c, c), 0)
    mm = _iota2((c, c), 1)
    l_incl = jnp.where(ii >= mm, g_row, 0.0)
    l_excl = jnp.where(mm > ii, g_row, 0.0)
    m2 = _iota2((c, 256), 0)
    j2 = _iota2((c, 256), 1)
    rhs1 = jnp.where(j2 >= 128, 1.0, jnp.where(m2 > j2, 1.0, 0.0)).astype(BF16)
    ones = jnp.ones((c, 128), BF16)
    out1 = _dot_sel(l_incl, rhs1)
    return out1[:, :c], out1[:, 128:], _dot_sel(l_excl, ones)


def _col_bcast(row, c):
    ii = _iota2((c, c), 0)
    mm = _iota2((c, c), 1)
    return _dot_sel(jnp.where(ii == mm, row, 0.0), jnp.ones((c, 128), BF16))


def _softplus(x):
    return jnp.maximum(x, 0.0) + jnp.log1p(jnp.exp(-jnp.abs(x)))


def _silu(x):
    return x * jax.nn.sigmoid(x)


def _conv_block(x_ref, e_ref, w_ref, rows):
    e_ref[8:rows + 8, :] = x_ref[...]
    w = w_ref[...]
    return (e_ref[8:rows + 8, :] * w[3:4, :] + e_ref[7:rows + 7, :] * w[2:3, :]
            + e_ref[6:rows + 6, :] * w[1:2, :] + e_ref[5:rows + 5, :] * w[0:1, :])


def _conv_carry(e_ref, rows):
    e_ref[0:8, :] = e_ref[rows:rows + 8, :]


def _ret_body(lg_ref, cd_ref, mix_ref, q_ref, k_ref, v_ref, g_ref, cos_ref, sin_ref, o_ref, s_ref, *, rows, dk):
    del mix_ref
    h = pl.program_id(1)
    lg = lg_ref[h]

    @pl.when(pl.program_id(2) == 0)
    def _():
        s_ref[...] = jnp.zeros_like(s_ref)

    cos = cos_ref[...]
    sin = sin_ref[...]
    half = dk // 2

    def rot(x):
        x1 = x[:, :half]
        x2 = x[:, half:]
        return jnp.concatenate([x1 * cos - x2 * sin, x1 * sin + x2 * cos], axis=-1)

    q = rot(q_ref[...])
    k = rot(k_ref[...]) * (dk ** -0.5)
    v = v_ref[...]
    c = CH
    ii = _iota2((c, c), 0)
    jj = _iota2((c, c), 1)
    dmask = jnp.where(ii >= jj, jnp.exp(lg * jnp.maximum(ii - jj, 0).astype(F32)), 0.0)
    ti = _iota2((c, 1), 0).astype(F32)
    q_dec = jnp.exp((ti + 1.0) * lg)
    k_dec = jnp.exp((c - 1.0 - ti) * lg)
    s = s_ref[...]
    for ch in range(rows // c):
        sl = slice(ch * c, (ch + 1) * c)
        qc, kc, vc = q[sl], k[sl], v[sl]
        att = _dot_nt(qc, kc) * dmask
        o = _dot(att, vc) + _dot(qc * q_dec, s)
        s = s * cd_ref[h] + _dot_tn(kc * k_dec, vc)
        oc = o - jnp.mean(o, -1, keepdims=True)
        on = oc * lax.rsqrt(jnp.mean(oc * oc, -1, keepdims=True) + NORM_EPS)
        o_ref[sl, :] = _silu(g_ref[sl, :]) * on
    s_ref[...] = s


def _retention_prompt(mix, proj, cos, sin, log_gamma, *, rows=256):
    ntb = SEQ // rows
    heads, dk = RET_HEADS, RET_DK
    cdec = jnp.exp(CH * log_gamma)
    smem = pl.BlockSpec(memory_space=pltpu.SMEM)

    def blk(part):
        return pl.BlockSpec((rows, dk), lambda b, h, t: (b * ntb + t, part * heads + h))

    return pl.pallas_call(
        functools.partial(_ret_body, rows=rows, dk=dk),
        grid=(BATCH, heads, ntb),
        in_specs=[smem, smem, pl.BlockSpec(memory_space=pl.ANY), blk(0), blk(1), blk(2), blk(3),
                  pl.BlockSpec((rows, dk // 2), lambda b, h, t: (t, 0)),
                  pl.BlockSpec((rows, dk // 2), lambda b, h, t: (t, 0))],
        out_specs=[pl.BlockSpec((rows, dk), lambda b, h, t: (b * ntb + t, h)),
                   pl.BlockSpec((None, None, dk, dk), lambda b, h, t: (b, h, 0, 0))],
        out_shape=[jax.ShapeDtypeStruct(mix.shape, F32),
                   jax.ShapeDtypeStruct((BATCH, heads, dk, dk), F32)],
        input_output_aliases={2: 0},
        compiler_params=pltpu.CompilerParams(dimension_semantics=("arbitrary",) * 3, vmem_limit_bytes=VMEM_LIMIT),
    )(log_gamma, cdec, mix, proj, proj, proj, proj, cos, sin)


def _ssd_body(alog_ref, dtb_ref, dsk_ref, mix_ref, x_ref, bm_ref, cm_ref, z_ref, dt_ref,
              wx_ref, wb_ref, wc_ref, bx_ref, bb_ref, bc_ref, nw_ref,
              o_ref, st_ref, cx_ref, cb_ref, cc_ref, ex, eb, ec, s_t, *, rows, hpg, hd):
    del mix_ref
    g = pl.program_id(1)
    tb = pl.program_id(2)
    last_tb = pl.num_programs(2) - 1
    c = CH
    nch = rows // c

    @pl.when(tb == 0)
    def _():
        s_t[...] = jnp.zeros_like(s_t)
        ex[0:8, :] = jnp.zeros((8, ex.shape[1]), F32)
        eb[0:8, :] = jnp.zeros((8, eb.shape[1]), F32)
        ec[0:8, :] = jnp.zeros((8, ec.shape[1]), F32)

    xs = _silu(_conv_block(x_ref, ex, wx_ref, rows) + bx_ref[...])
    bm = _silu(_conv_block(bm_ref, eb, wb_ref, rows) + bb_ref[...])
    cm = _silu(_conv_block(cm_ref, ec, wc_ref, rows) + bc_ref[...])

    @pl.when(tb == last_tb)
    def _():
        cx_ref[...] = ex[rows + 5:rows + 8, :]
        cb_ref[...] = eb[rows + 5:rows + 8, :]
        cc_ref[...] = ec[rows + 5:rows + 8, :]

    _conv_carry(ex, rows)
    _conv_carry(eb, rows)
    _conv_carry(ec, rows)

    ii = _iota2((c, c), 0)
    jj = _iota2((c, c), 1)
    incl = ii >= jj
    nw = nw_ref[...]
    for ch in range(nch):
        sl = slice(ch * c, (ch + 1) * c)
        xc = xs[sl]
        bc = bm[sl]
        cc = cm[sl]
        cbm = _dot_nt(cc, bc)
        st = s_t[...]
        ystate = _dot(cc, st)
        ys, wxs, decs = [], [], []
        for r in range(hpg):
            h = g * hpg + r
            dt_row = _softplus(dt_ref[r, ch] + dtb_ref[h])
            a_vec = -jnp.exp(jnp.zeros((1, c), F32) + alog_ref[h])
            dmat, cs_cb, rest_cb = _decay_terms(dt_row * a_vec, c)
            dt_cb = _col_bcast(dt_row, c)
            lm = jnp.where(incl, jnp.exp(jnp.where(incl, dmat, 0.0)), 0.0)
            x_r = xc[:, r * hd:(r + 1) * hd]
            y_r = _dot(cbm * lm * dt_row, x_r) + ystate[:, r * hd:(r + 1) * hd] * jnp.exp(cs_cb[:, :hd])
            ys.append(y_r + dsk_ref[h] * x_r)
            wxs.append(x_r * (jnp.exp(rest_cb[:, :hd]) * dt_cb[:, :hd]))
            decs.append(jnp.exp(cs_cb[c - 1:c, :hd]))
        s_t[...] = st * jnp.concatenate(decs, axis=-1) + _dot_tn(bc, jnp.concatenate(wxs, axis=-1))
        y = jnp.concatenate(ys, axis=-1) * _silu(z_ref[sl, :])
        o_ref[sl, :] = y * lax.rsqrt(jnp.mean(y * y, -1, keepdims=True) + NORM_EPS) * nw

    @pl.when(tb == last_tb)
    def _():
        st_ref[...] = s_t[...].T


def _ssd_prompt(mix, proj, dt_t, conv_w, conv_b, a_log, dt_bias, d_skip, norm_w, *, rows=128):
    ntb = SEQ // rows
    nch = rows // CH
    groups, hpg, hd, n_state = SSM_GROUPS, SSM_HEADS // SSM_GROUPS, SSM_P, SSM_N
    gw = hpg * hd
    z_col = (4 * RET_HEADS * RET_DK) // gw
    x_col = z_col + SSM_INNER // gw
    b_col = (x_col * gw + SSM_INNER) // n_state
    c_col = b_col + groups
    wb_col = SSM_INNER // n_state
    wc_col = wb_col + groups
    out_col = (RET_HEADS * RET_DV) // gw
    smem = pl.BlockSpec(memory_space=pltpu.SMEM)
    in_specs = [
        smem, smem, smem,
        pl.BlockSpec(memory_space=pl.ANY),
        pl.BlockSpec((rows, gw), lambda b, g, t: (b * ntb + t, x_col + g)),
        pl.BlockSpec((rows, n_state), lambda b, g, t: (b * ntb + t, b_col + g)),
        pl.BlockSpec((rows, n_state), lambda b, g, t: (b * ntb + t, c_col + g)),
        pl.BlockSpec((rows, gw), lambda b, g, t: (b * ntb + t, z_col + g)),
        pl.BlockSpec((hpg, nch, 1, CH), lambda b, g, t: (g, b * ntb + t, 0, 0)),
        pl.BlockSpec((CONV_W, gw), lambda b, g, t: (0, g)),
        pl.BlockSpec((CONV_W, n_state), lambda b, g, t: (0, wb_col + g)),
        pl.BlockSpec((CONV_W, n_state), lambda b, g, t: (0, wc_col + g)),
        pl.BlockSpec((1, gw), lambda b, g, t: (0, g)),
        pl.BlockSpec((1, n_state), lambda b, g, t: (0, wb_col + g)),
        pl.BlockSpec((1, n_state), lambda b, g, t: (0, wc_col + g)),
        pl.BlockSpec((1, gw), lambda b, g, t: (0, g)),
    ]
    out_specs = [
        pl.BlockSpec((rows, gw), lambda b, g, t: (b * ntb + t, out_col + g)),
        pl.BlockSpec((None, gw, n_state), lambda b, g, t: (b, g, 0)),
        pl.BlockSpec((None, CONV_W - 1, gw), lambda b, g, t: (b, 0, g)),
        pl.BlockSpec((None, CONV_W - 1, n_state), lambda b, g, t: (b, 0, g)),
        pl.BlockSpec((None, CONV_W - 1, n_state), lambda b, g, t: (b, 0, g)),
    ]
    out_shape = [
        jax.ShapeDtypeStruct(mix.shape, F32),
        jax.ShapeDtypeStruct((BATCH, SSM_INNER, n_state), F32),
        jax.ShapeDtypeStruct((BATCH, CONV_W - 1, SSM_INNER), F32),
        jax.ShapeDtypeStruct((BATCH, CONV_W - 1, groups * n_state), F32),
        jax.ShapeDtypeStruct((BATCH, CONV_W - 1, groups * n_state), F32),
    ]
    scratch = [pltpu.VMEM((rows + 8, gw), F32), pltpu.VMEM((rows + 8, n_state), F32),
               pltpu.VMEM((rows + 8, n_state), F32), pltpu.VMEM((n_state, gw), F32)]
    return pl.pallas_call(
        functools.partial(_ssd_body, rows=rows, hpg=hpg, hd=hd),
        grid=(BATCH, groups, ntb),
        in_specs=in_specs, out_specs=out_specs, out_shape=out_shape, scratch_shapes=scratch,
        input_output_aliases={3: 0},
        compiler_params=pltpu.CompilerParams(dimension_semantics=("arbitrary",) * 3, vmem_limit_bytes=VMEM_LIMIT),
    )(a_log, dt_bias, d_skip, mix, proj, proj, proj, proj, dt_t, conv_w, conv_w, conv_w,
      conv_b, conv_b, conv_b, norm_w)


def _inv_unit_lower(a, c):
    ii = _iota2((c, c), 0)
    jj = _iota2((c, c), 1)
    n = -a
    p = jnp.where(ii == jj, 1.0, 0.0) + n
    for _ in range(c.bit_length() - 2):
        n = _dot_hi(n, n)
        p = p + _dot_hi(p, n)
    return p


def _gdn_body(alog_ref, dtb_ref, mix_ref, q_ref, k_ref, v_ref, z_ref, a_ref, b_ref, wq_ref, wk_ref, wv_ref, nw_ref,
              o_ref, s_ref, cq_ref, ck_ref, cv_ref, eq, ek, ev, *, rows, rep, dk, dv):
    del mix_ref
    hk = pl.program_id(1)
    tb = pl.program_id(2)
    last_tb = pl.num_programs(2) - 1
    c = CH
    nch = rows // c

    @pl.when(tb == 0)
    def _():
        s_ref[...] = jnp.zeros_like(s_ref)
        eq[0:8, :] = jnp.zeros((8, eq.shape[1]), F32)
        ek[0:8, :] = jnp.zeros((8, ek.shape[1]), F32)
        ev[0:8, :] = jnp.zeros((8, ev.shape[1]), F32)

    q = _silu(_conv_block(q_ref, eq, wq_ref, rows))
    k = _silu(_conv_block(k_ref, ek, wk_ref, rows))
    v = _silu(_conv_block(v_ref, ev, wv_ref, rows))

    @pl.when(tb == last_tb)
    def _():
        cq_ref[...] = eq[rows + 5:rows + 8, :]
        ck_ref[...] = ek[rows + 5:rows + 8, :]
        cv_ref[...] = ev[rows + 5:rows + 8, :]

    _conv_carry(eq, rows)
    _conv_carry(ek, rows)
    _conv_carry(ev, rows)

    q = q * lax.rsqrt(jnp.sum(q * q, -1, keepdims=True) + 1e-6) * (dk ** -0.5)
    k = k * lax.rsqrt(jnp.sum(k * k, -1, keepdims=True) + 1e-6)

    c2 = 2 * c
    ii = _iota2((c2, c2), 0)
    jj = _iota2((c2, c2), 1)
    same = (ii >= c) == (jj >= c)
    ti = ii & (c - 1)
    tj = jj & (c - 1)
    incl = same & (ti >= tj)
    strict = same & (ti > tj)
    later = same & (tj > ti)
    diag = ii == jj
    top = ii < c
    col2 = _iota2((c2, 2 * c2), 1)
    row2 = _iota2((c2, 2 * c2), 0)
    sel1 = jnp.where(col2 >= c2, 1.0, jnp.where(((row2 >= c) == (col2 >= c)) & ((row2 & (c - 1)) > (col2 & (c - 1))),
                                                 1.0, 0.0)).astype(BF16)
    sel2 = jnp.where((_iota2((2 * c2, 2 * c2), 0) >= c2) == (_iota2((2 * c2, 2 * c2), 1) >= c2), 1.0, 0.0).astype(BF16)
    eye = jnp.where(diag, 1.0, 0.0)
    lane = _iota2((1, c2), 1)
    alog = jnp.where(lane < c, alog_ref[hk * 2], alog_ref[hk * 2 + 1])
    dtb = jnp.where(lane < c, dtb_ref[hk * 2], dtb_ref[hk * 2 + 1])
    neg_a = -jnp.exp(alog)
    nw = nw_ref[...]

    pre = []
    for ch in range(nch):
        sl = slice(ch * c, (ch + 1) * c)
        q2 = jnp.concatenate([q[sl], q[sl]], axis=0)
        k2 = jnp.concatenate([k[sl], k[sl]], axis=0)
        v2 = jnp.concatenate([v[sl, :dv], v[sl, dv:]], axis=0)
        g_row = neg_a * _softplus(a_ref[ch] + dtb)
        beta_row = jax.nn.sigmoid(b_ref[ch])
        out1 = _dot_sel(jnp.where(incl, g_row, 0.0), sel1)
        out2 = _dot_sel(jnp.concatenate([jnp.where(later, g_row, 0.0), jnp.where(diag, beta_row, 0.0)], axis=1), sel2)
        gcum_cb, rest_cb, beta_cb = out1[:, c2:], out2[:, :c2], out2[:, c2:]
        decay = jnp.where(incl, jnp.exp(jnp.where(incl, out1[:, :c2], 0.0)), 0.0)
        eg = jnp.exp(gcum_cb)
        kb2 = k2 * beta_cb
        pre.append(dict(
            n=-jnp.where(strict, _dot_nt(kb2, k2) * decay, 0.0),
            rhs=jnp.concatenate([v2 * beta_cb, kb2 * eg], axis=1),
            qe=q2 * eg,
            qk=_dot_nt(q2, k2) * decay,
            kw=k2 * jnp.exp(rest_cb),
            declast=jnp.concatenate([eg[c - 1:c, :], eg[c2 - 1:c2, :]], axis=1)))

    ps = [eye + d["n"] for d in pre]
    ms = [_dot_hi(d["n"], d["n"]) for d in pre]
    n_steps = c.bit_length() - 2
    for step in range(n_steps):
        if step < n_steps - 1:
            rs = [_dot_hi(jnp.concatenate([p, m], axis=0), m) for p, m in zip(ps, ms)]
            ps = [p + r[:c2] for p, r in zip(ps, rs)]
            ms = [r[c2:] for r in rs]
        else:
            ps = [p + _dot_hi(p, m) for p, m in zip(ps, ms)]
    sols = [_dot_hi(p, d["rhs"]) for p, d in zip(ps, pre)]

    s_cat = jnp.concatenate([s_ref[0], s_ref[1]], axis=1)
    for ch in range(nch):
        sl = slice(ch * c, (ch + 1) * c)
        d = pre[ch]
        sol = sols[ch]
        r = _dot(jnp.concatenate([sol[:, dv:], d["qe"]], axis=0), s_cat)
        u = sol[:, :dv] - jnp.where(top, r[:c2, :dv], r[:c2, dv:])
        o = jnp.where(top, r[c2:, :dv], r[c2:, dv:]) + _dot(d["qk"], u)
        u_bd = jnp.concatenate([jnp.where(top, u, 0.0), jnp.where(top, 0.0, u)], axis=1)
        s_cat = s_cat * d["declast"] + _dot_tn(d["kw"], u_bd)
        zz = jnp.concatenate([z_ref[sl, :dv], z_ref[sl, dv:]], axis=0)
        res = o * lax.rsqrt(jnp.mean(o * o, -1, keepdims=True) + NORM_EPS) * nw * _silu(zz)
        o_ref[sl, :dv] = res[:c]
        o_ref[sl, dv:] = res[c:]
    s_ref[0] = s_cat[:, :dv]
    s_ref[1] = s_cat[:, dv:]


def _gdn_prompt(mix, proj, a_t, b_t, conv_w, a_log, dt_bias, norm_w, *, rows=256):
    hk, hv, dk, dv = GDN_HK, GDN_HV, GDN_DK, GDN_DV
    rep = hv // hk
    assert rep == 2 and dk == dv
    ntb = SEQ // rows
    nch = rows // CH
    vw = rep * dv
    kcol = hk
    vcol = (2 * GDN_QK) // vw
    zcol = (2 * GDN_QK + GDN_VW) // vw
    smem = pl.BlockSpec(memory_space=pltpu.SMEM)
    in_specs = [
        smem, smem,
        pl.BlockSpec(memory_space=pl.ANY),
        pl.BlockSpec((rows, dk), lambda b, h, t: (b * ntb + t, h)),
        pl.BlockSpec((rows, dk), lambda b, h, t: (b * ntb + t, kcol + h)),
        pl.BlockSpec((rows, vw), lambda b, h, t: (b * ntb + t, vcol + h)),
        pl.BlockSpec((rows, vw), lambda b, h, t: (b * ntb + t, zcol + h)),
        pl.BlockSpec((None, nch, 1, rep * CH), lambda b, h, t: (h, b * ntb + t, 0, 0)),
        pl.BlockSpec((None, nch, 1, rep * CH), lambda b, h, t: (h, b * ntb + t, 0, 0)),
        pl.BlockSpec((CONV_W, dk), lambda b, h, t: (0, h)),
        pl.BlockSpec((CONV_W, dk), lambda b, h, t: (0, kcol + h)),
        pl.BlockSpec((CONV_W, vw), lambda b, h, t: (0, vcol + h)),
        pl.BlockSpec((1, dv), lambda b, h, t: (0, 0)),
    ]
    out_specs = [
        pl.BlockSpec((rows, vw), lambda b, h, t: (b * ntb + t, h)),
        pl.BlockSpec((None, rep, dk, dv), lambda b, h, t: (b, h, 0, 0)),
        pl.BlockSpec((None, CONV_W - 1, dk), lambda b, h, t: (b, 0, h)),
        pl.BlockSpec((None, CONV_W - 1, dk), lambda b, h, t: (b, 0, h)),
        pl.BlockSpec((None, CONV_W - 1, vw), lambda b, h, t: (b, 0, h)),
    ]
    out_shape = [
        jax.ShapeDtypeStruct(mix.shape, F32),
        jax.ShapeDtypeStruct((BATCH, hv, dk, dv), F32),
        jax.ShapeDtypeStruct((BATCH, CONV_W - 1, GDN_QK), F32),
        jax.ShapeDtypeStruct((BATCH, CONV_W - 1, GDN_QK), F32),
        jax.ShapeDtypeStruct((BATCH, CONV_W - 1, GDN_VW), F32),
    ]
    scratch = [pltpu.VMEM((rows + 8, dk), F32), pltpu.VMEM((rows + 8, dk), F32), pltpu.VMEM((rows + 8, vw), F32)]
    return pl.pallas_call(
        functools.partial(_gdn_body, rows=rows, rep=rep, dk=dk, dv=dv),
        grid=(BATCH, hk, ntb),
        in_specs=in_specs, out_specs=out_specs, out_shape=out_shape, scratch_shapes=scratch,
        input_output_aliases={2: 0},
        compiler_params=pltpu.CompilerParams(dimension_semantics=("arbitrary",) * 3, vmem_limit_bytes=VMEM_LIMIT),
    )(a_log, dt_bias, mix, proj, proj, proj, proj, a_t, b_t, conv_w, conv_w, conv_w, norm_w)


def _layer_norm(xf, g, b):
    xc = xf - jnp.mean(xf, -1, keepdims=True)
    var = jnp.mean(xc * xc, -1, keepdims=True)
    return xc * lax.rsqrt(var + NORM_EPS) * g + b


def _rms(xf):
    return xf * lax.rsqrt(jnp.mean(xf * xf, -1, keepdims=True) + NORM_EPS)


def _head_ln(xf):
    xc = xf - jnp.mean(xf, -1, keepdims=True)
    return xc * lax.rsqrt(jnp.mean(xc * xc, -1, keepdims=True) + NORM_EPS)


def _l2norm(xf):
    return xf * lax.rsqrt(jnp.sum(xf * xf, -1, keepdims=True) + 1e-6)


def _rotary(t, pos):
    half = t.shape[-1] // 2
    inv = ROPE_BASE ** (-jnp.arange(half, dtype=F32) / half)
    ang = pos.astype(F32)[:, None] * inv
    cos = jnp.cos(ang)[None, :, None, :]
    sin = jnp.sin(ang)[None, :, None, :]
    t1, t2 = t[..., :half], t[..., half:]
    return jnp.concatenate([t1 * cos - t2 * sin, t1 * sin + t2 * cos], -1)


def _chunk_len(L):
    return CHUNK if L % CHUNK == 0 else L


def _to_chunks(t, c):
    return t.reshape(t.shape[0], t.shape[1] // c, c, *t.shape[2:]).swapaxes(0, 1)


def _from_chunks(t):
    t = t.swapaxes(0, 1)
    return t.reshape(t.shape[0], t.shape[1] * t.shape[2], *t.shape[3:])


def _causal_conv(x, prev, w, b):
    L = x.shape[1]
    xp = jnp.concatenate([prev.astype(x.dtype), x], axis=1)
    y = xp[:, CONV_W - 1:] * w[CONV_W - 1]
    for i in range(CONV_W - 1):
        y = y + xp[:, i:i + L] * w[i]
    if b is not None:
        y = y + b
    return y, xp[:, L:]


def _retention_scan(q, k, v, s0, log_gamma):
    L = q.shape[1]
    c = _chunk_len(L)
    idx = jnp.arange(c, dtype=F32)
    diff = idx[:, None] - idx[None, :]
    dmask = jnp.exp(jnp.where((diff >= 0)[None], log_gamma[:, None, None] * diff[None], -jnp.inf))
    q_dec = jnp.exp((idx[:, None] + 1.0) * log_gamma)[:, :, None]
    k_dec = jnp.exp((c - 1.0 - idx)[:, None] * log_gamma)[:, :, None]
    c_dec = jnp.exp(c * log_gamma)[:, None, None]

    def step(s, inp):
        qc, kc, vc = inp
        att = jnp.einsum('bihd,bjhd->bhij', qc, kc) * dmask
        o = jnp.einsum('bhij,bjhe->bihe', att, vc) + jnp.einsum('bihd,bhde->bihe', qc * q_dec, s)
        s = s * c_dec + jnp.einsum('bjhd,bjhe->bhde', kc * k_dec, vc)
        return s, o

    s, o = lax.scan(step, s0, (_to_chunks(q, c), _to_chunks(k, c), _to_chunks(v, c)))
    return _from_chunks(o), s


def _ssd_scan(x, dt, a, bm, cm, s0):
    L = x.shape[1]
    c = _chunk_len(L)
    ar = jnp.arange(c)
    mask5 = (ar[:, None] >= ar[None, :])[None, :, :, None, None]

    def step(s, inp):
        xc, dtc, bc, cc = inp
        cs = jnp.cumsum(dtc * a, axis=1)
        lmat = jnp.exp(jnp.where(mask5, cs[:, :, None] - cs[:, None, :], -jnp.inf))
        cb = jnp.einsum('bign,bjgn->bijg', cc, bc)
        y = jnp.einsum('bijg,bijgr,bjgr,bjgrp->bigrp', cb, lmat, dtc, xc)
        y = y + jnp.einsum('bign,bgrpn->bigrp', cc, s) * jnp.exp(cs)[..., None]
        last = cs[:, -1]
        w_end = jnp.exp(last[:, None] - cs) * dtc
        s = s * jnp.exp(last)[..., None, None] + jnp.einsum('bjgr,bjgn,bjgrp->bgrpn', w_end, bc, xc)
        return s, y

    s, y = lax.scan(step, s0, (_to_chunks(x, c), _to_chunks(dt, c), _to_chunks(bm, c), _to_chunks(cm, c)))
    return _from_chunks(y), s


def _gdn_scan(q, k, v, g, beta, s0):
    L = q.shape[1]
    dv = v.shape[-1]
    c = _chunk_len(L)
    ar = jnp.arange(c)
    incl = ar[:, None] >= ar[None, :]
    strict = ar[:, None] > ar[None, :]
    eye = jnp.eye(c, dtype=F32)

    def step(s, inp):
        qc, kc, vc, gc, bc = inp
        gcum = jnp.cumsum(gc, axis=1)
        gh = gcum.swapaxes(1, 2)
        decay = jnp.exp(jnp.where(incl, gh[..., :, None] - gh[..., None, :], -jnp.inf))
        kb = kc * bc[..., None]
        a_mat = jnp.where(strict, jnp.einsum('bihd,bjhd->bhij', kb, kc) * decay, 0.0)
        rhs = jnp.concatenate([(vc * bc[..., None]).swapaxes(1, 2),
                               (kb * jnp.exp(gcum)[..., None]).swapaxes(1, 2)], -1)
        sol = lax.linalg.triangular_solve(a_mat + eye, rhs, left_side=True, lower=True, unit_diagonal=True)
        u = sol[..., :dv] - jnp.einsum('bhid,bhde->bhie', sol[..., dv:], s)
        qk = jnp.einsum('bihd,bjhd->bhij', qc, kc) * decay
        o = jnp.einsum('bihd,bhde->bihe', qc * jnp.exp(gcum)[..., None], s) + jnp.einsum('bhij,bhje->bihe', qk, u)
        g_last = gcum[:, -1]
        s = s * jnp.exp(g_last)[..., None, None] + jnp.einsum(
            'bjhd,bhje->bhde', kc * jnp.exp(g_last[:, None] - gcum)[..., None], u)
        return s, o

    xs = (_to_chunks(q, c), _to_chunks(k, c), _to_chunks(v, c), _to_chunks(g, c), _to_chunks(beta, c))
    s, o = lax.scan(step, s0, xs)
    return _from_chunks(o), s


def _even_core(proj, dt_raw, pos, s_ret, s_ssm, conv_prev, conv_w, conv_b, dt_bias, a_log, d_skip, norm_w):
    bsz, L, _ = proj.shape
    q, k, v, g, z, xbc = _split(proj, EVEN_SPLITS[:-1])
    q = _rotary(q.reshape(bsz, L, RET_HEADS, RET_DK), pos)
    k = _rotary(k.reshape(bsz, L, RET_HEADS, RET_DK), pos) * RET_DK ** -0.5
    v = v.reshape(bsz, L, RET_HEADS, RET_DV)
    log_gamma = jnp.log1p(-jnp.exp2(-5.0 - jnp.arange(RET_HEADS, dtype=F32)))
    o_ret, s_ret_new = _retention_scan(q, k, v, s_ret, log_gamma)
    o_ret = jax.nn.silu(g) * _head_ln(o_ret).reshape(bsz, L, -1)
    xbc, conv_new = _causal_conv(xbc, conv_prev, conv_w, conv_b)
    xs, bm, cm = _split(jax.nn.silu(xbc), (SSM_INNER, SSM_GROUPS * SSM_N, SSM_GROUPS * SSM_N))
    r = SSM_HEADS // SSM_GROUPS
    xs = xs.reshape(bsz, L, SSM_GROUPS, r, SSM_P)
    dt = jax.nn.softplus(dt_raw + dt_bias).reshape(bsz, L, SSM_GROUPS, r)
    a = -jnp.exp(a_log).reshape(SSM_GROUPS, r)
    y, s_ssm_new = _ssd_scan(xs, dt, a, bm.reshape(bsz, L, SSM_GROUPS, SSM_N), cm.reshape(bsz, L, SSM_GROUPS, SSM_N),
                             s_ssm.reshape(bsz, SSM_GROUPS, r, SSM_P, SSM_N))
    y = y + d_skip.reshape(SSM_GROUPS, r, 1) * xs
    y = y.reshape(bsz, L, SSM_INNER) * jax.nn.silu(z)
    y = _rms(y.reshape(bsz, L, SSM_GROUPS, -1)).reshape(bsz, L, SSM_INNER) * norm_w
    mix = jnp.concatenate([o_ret, y], -1)
    return mix, s_ret_new, s_ssm_new.reshape(bsz, SSM_HEADS, SSM_P, SSM_N), conv_new


def _odd_core(proj, ab, s_gdn, conv_prev, conv_w, dt_bias, a_log, norm_w):
    bsz, L, _ = proj.shape
    qkv, z = _split(proj, ODD_SPLITS[:2])
    a, b = _split(ab, ODD_SPLITS[2:])
    qkv, conv_new = _causal_conv(qkv, conv_prev, conv_w, None)
    q, k, v = _split(jax.nn.silu(qkv), (GDN_QK, GDN_QK, GDN_VW))
    rep = GDN_HV // GDN_HK
    q = jnp.repeat(_l2norm(q.reshape(bsz, L, GDN_HK, GDN_DK)), rep, axis=2) * GDN_DK ** -0.5
    k = jnp.repeat(_l2norm(k.reshape(bsz, L, GDN_HK, GDN_DK)), rep, axis=2)
    v = v.reshape(bsz, L, GDN_HV, GDN_DV)
    g = -jnp.exp(a_log) * jax.nn.softplus(a + dt_bias)
    beta = jax.nn.sigmoid(b)
    o, s_new = _gdn_scan(q, k, v, g, beta, s_gdn)
    o = _rms(o) * norm_w * jax.nn.silu(z.reshape(bsz, L, GDN_HV, GDN_DV))
    return o.reshape(bsz, L, GDN_VW), s_new, conv_new


def _post_block(x, mix_out, p, layer, ln1_g, ln1_b, ln2_g, ln2_b, router_w, router_b, we_gate, we_up, we_down,
                ws_gate, ws_up, ws_down, ple_proj, ple_gate):
    n_tok = x.shape[0]
    x, xb, idx, wts = _ln_router(x, mix_out, ln1_g, ln1_b, router_w, router_b, layer, tm=520)
    shared = _shared_expert(xb, ws_gate, ws_up, ws_down, layer, tm=520)
    rows_tok, rows_gate, pair_row, be, n_used = _route_plan(idx, wts, n_tok)
    x_pad = jnp.concatenate([xb, jnp.zeros((1, D_MODEL), BF16)], 0)
    y_rows = _experts(x_pad[rows_tok], rows_gate, be, n_used, we_gate, we_up, we_down, layer)
    x = _combine_ln(y_rows, pair_row, shared, x, ln2_g, ln2_b, layer, tm=104)
    return _ple(x, p, ple_gate, ple_proj, layer, tm=1040, tn=512)


def kernel(x_prompt, x_sample, state_ret, state_ssm, state_ssm_conv, state_gdn, state_gdn_conv, p_prompt, p_sample, ev_w_in, ev_w_out, ssm_conv_w, ssm_conv_b, ssm_dt_bias, ssm_a_log, ssm_d, ssm_norm_w, od_w_in, od_w_out, gdn_conv_w, gdn_dt_bias, gdn_a_log, gdn_norm_w, ln1_g, ln1_b, ln2_g, ln2_b, router_w, router_b, exp_w_gate, exp_w_up, exp_w_down, sh_w_gate, sh_w_up, sh_w_down, ple_proj, ple_gate):
    bp, lp = x_prompt.shape[0], x_prompt.shape[1]
    ls = x_sample.shape[1]
    pos_p = jnp.arange(lp, dtype=jnp.int32)
    pos_s = PAST_LEN + jnp.arange(ls, dtype=jnp.int32)
    x = jnp.concatenate([x_prompt.reshape(N_PROMPT, D_MODEL), x_sample.reshape(DEC_BATCH, D_MODEL)], 0)
    p_all = jnp.concatenate([p_prompt.reshape(DEPTH, N_PROMPT, D_PLE), p_sample.reshape(DEPTH, DEC_BATCH, D_PLE)], 1)
    ret_p, ret_s, ssm_p, ssm_s, sconv_p, sconv_s = [], [], [], [], [], []
    gdn_p, gdn_s, gconv_p, gconv_s = [], [], [], []
    half = RET_DK // 2
    ang = pos_p.astype(F32)[:, None] * (ROPE_BASE ** (-jnp.arange(half, dtype=F32) / half))
    cos_p, sin_p = jnp.cos(ang), jnp.sin(ang)
    log_gamma = jnp.log1p(-jnp.exp2(-5.0 - jnp.arange(RET_HEADS, dtype=F32)))
    for i in range(DEPTH):
        j = i // 2
        if i % 2 == 0:
            proj = _dense(x, ev_w_in, (j,), n_cols=EVEN_MAIN, tm=1040, tn=512)
            dt_raw = _dense(x, ev_w_in[j, :, EVEN_MAIN:], tm=1040, tn=SSM_HEADS)
            w = (ssm_conv_w[j], ssm_conv_b[j], ssm_dt_bias[j], ssm_a_log[j], ssm_d[j], ssm_norm_w[j])
            mix, r1 = _retention_prompt(jnp.zeros((N_TOK, EVEN_MIX), F32), proj, cos_p, sin_p, log_gamma)
            dt_t = dt_raw[:N_PROMPT].T.reshape(SSM_HEADS, N_PROMPT // CH, 1, CH)
            mix, r2, cx, cb, cc = _ssd_prompt(mix, proj, dt_t, ssm_conv_w[j], ssm_conv_b[j].reshape(1, -1),
                                              ssm_a_log[j], ssm_dt_bias[j], ssm_d[j], ssm_norm_w[j].reshape(1, -1))
            r2 = r2.reshape(bp, SSM_HEADS, SSM_P, SSM_N)
            r3 = jnp.concatenate([cx, cb, cc], -1)
            ms, q1, q2, q3 = _even_core(proj[N_PROMPT:].reshape(DEC_BATCH, ls, -1),
                                        dt_raw[N_PROMPT:].reshape(DEC_BATCH, ls, -1),
                                        pos_s, state_ret[j], state_ssm[j], state_ssm_conv[j], *w)
            ret_p.append(r1); ssm_p.append(r2); sconv_p.append(r3)
            ret_s.append(q1); ssm_s.append(q2); sconv_s.append(q3)
            mix = lax.dynamic_update_slice(mix, ms.reshape(DEC_BATCH, -1), (N_PROMPT, 0))
            mix_out = _dense(mix, ev_w_out, (j,), tm=520, tn=512)
        else:
            proj = _dense(x, od_w_in, (j,), n_cols=ODD_MAIN, tm=1040, tn=512)
            ab = _dense(x, od_w_in[j, :, ODD_MAIN:], tm=1040, tn=2 * GDN_HV)
            w = (gdn_conv_w[j], gdn_dt_bias[j], gdn_a_log[j], gdn_norm_w[j])
            ab_t = ab[:N_PROMPT].reshape(N_PROMPT // CH, CH, 2, GDN_HK, 2).transpose(2, 3, 0, 4, 1)
            ab_t = ab_t.reshape(2, GDN_HK, N_PROMPT // CH, 1, 2 * CH)
            mix, r1, cq, ck, cv = _gdn_prompt(jnp.zeros((N_TOK, GDN_VW), F32), proj, ab_t[0], ab_t[1], gdn_conv_w[j],
                                              gdn_a_log[j], gdn_dt_bias[j], gdn_norm_w[j].reshape(1, -1))
            r2 = jnp.concatenate([cq, ck, cv], -1)
            ms, q1, q2 = _odd_core(proj[N_PROMPT:].reshape(DEC_BATCH, ls, -1), ab[N_PROMPT:].reshape(DEC_BATCH, ls, -1),
                                   state_gdn[j], state_gdn_conv[j], *w)
            gdn_p.append(r1); gconv_p.append(r2)
            gdn_s.append(q1); gconv_s.append(q2)
            mix = lax.dynamic_update_slice(mix, ms.reshape(DEC_BATCH, -1), (N_PROMPT, 0))
            mix_out = _dense(mix, od_w_out, (j,), tm=520, tn=512)
        x = _post_block(x, mix_out, p_all, i, ln1_g, ln1_b, ln2_g, ln2_b, router_w, router_b,
                        exp_w_gate, exp_w_up, exp_w_down, sh_w_gate, sh_w_up, sh_w_down, ple_proj, ple_gate)
    xp = x[:N_PROMPT].reshape(bp, lp, D_MODEL)
    xs = x[N_PROMPT:].reshape(DEC_BATCH, ls, D_MODEL)
    return (xp, xs, jnp.stack(ret_p), jnp.stack(ret_s), jnp.stack(ssm_p), jnp.stack(ssm_s),
            jnp.stack(sconv_p), jnp.stack(sconv_s), jnp.stack(gdn_p), jnp.stack(gdn_s),
            jnp.stack(gconv_p), jnp.stack(gconv_s))
```

```python
import functools
import math

import jax
import jax.numpy as jnp
from jax import lax
from jax.experimental import pallas as pl
from jax.experimental.pallas import tpu as pltpu

F32 = jnp.float32
BF16 = jnp.bfloat16

D_MODEL = 2048
BATCH = 4
SEQ = 2048
DEPTH = 4
DEC_BATCH = 128
PAST_LEN = 16384
CHUNK = 64
CONV_W = 4
NORM_EPS = 1e-5
RET_HEADS = 8
RET_DK = D_MODEL // RET_HEADS
RET_DV = D_MODEL // RET_HEADS
ROPE_BASE = 10000.0
SSM_HEADS = 32
SSM_P = D_MODEL // SSM_HEADS
SSM_INNER = SSM_HEADS * SSM_P
SSM_GROUPS = 4
SSM_N = 128
SSM_CONV_CH = SSM_INNER + 2 * SSM_GROUPS * SSM_N
GDN_HK = 16
GDN_HV = 32
GDN_DK = 128
GDN_DV = 128
GDN_QK = GDN_HK * GDN_DK
GDN_VW = GDN_HV * GDN_DV
GDN_CONV_CH = 2 * GDN_QK + GDN_VW
EVEN_SPLITS = (RET_HEADS * RET_DK, RET_HEADS * RET_DK, RET_HEADS * RET_DV, RET_HEADS * RET_DV,
               SSM_INNER, SSM_CONV_CH, SSM_HEADS)
EVEN_MAIN = sum(EVEN_SPLITS[:-1])
EVEN_MIX = RET_HEADS * RET_DV + SSM_INNER
ODD_SPLITS = (GDN_CONV_CH, GDN_VW, GDN_HV, GDN_HV)
ODD_MAIN = GDN_CONV_CH + GDN_VW
N_EXPERTS = 64
TOP_K = 8
D_EXPERT = D_MODEL // 4
ROUTED_SCALE = 2.5
D_PLE = 256
ALPHA = (2 * DEPTH) ** 0.25

N_PROMPT = BATCH * SEQ
N_TOK = N_PROMPT + DEC_BATCH
EXPERT_ROWS = 256
VMEM_LIMIT = 56 * 1024 * 1024


def _split(t, sizes):
    out, start = [], 0
    for s in sizes:
        out.append(t[..., start:start + s])
        start += s
    return out


def _dense_body(x_ref, w_ref, o_ref, wb_ref):
    @pl.when(pl.program_id(1) == 0)
    def _():
        wb_ref[...] = w_ref[...].astype(BF16)

    o_ref[...] = jnp.dot(x_ref[...].astype(BF16), wb_ref[...], preferred_element_type=F32)


def _dense(x, w, lead=(), *, n_cols=None, tm, tn):
    m, k = x.shape
    n_cols = w.shape[-1] if n_cols is None else n_cols
    assert m % tm == 0 and n_cols % tn == 0
    nl = len(lead)
    w_block = (None,) * nl + (k, tn)
    return pl.pallas_call(
        _dense_body,
        grid=(n_cols // tn, m // tm),
        in_specs=[pl.BlockSpec((tm, k), lambda j, i: (i, 0)),
                  pl.BlockSpec(w_block, lambda j, i: lead + (0, j))],
        out_specs=pl.BlockSpec((tm, tn), lambda j, i: (i, j)),
        out_shape=jax.ShapeDtypeStruct((m, n_cols), F32),
        scratch_shapes=[pltpu.VMEM((k, tn), BF16)],
        compiler_params=pltpu.CompilerParams(dimension_semantics=("arbitrary", "arbitrary"),
                                             vmem_limit_bytes=VMEM_LIMIT),
    )(x, w)


def _dense_f32_body(x_ref, w_ref, o_ref):
    o_ref[...] = jnp.dot(x_ref[...].astype(BF16), w_ref[...].astype(BF16), preferred_element_type=F32)


def _dense_f32(x, w, lead=(), *, tm):
    m, k = x.shape
    n = w.shape[-1]
    nl = len(lead)
    return pl.pallas_call(
        _dense_f32_body,
        grid=(m // tm,),
        in_specs=[pl.BlockSpec((tm, k), lambda i: (i, 0)),
                  pl.BlockSpec((None,) * nl + (k, n), lambda i: lead + (0, 0))],
        out_specs=pl.BlockSpec((tm, n), lambda i: (i, 0)),
        out_shape=jax.ShapeDtypeStruct((m, n), F32),
        compiler_params=pltpu.CompilerParams(dimension_semantics=("arbitrary",),
                                             vmem_limit_bytes=VMEM_LIMIT),
    )(x, w)


def _ln_router_body(x_ref, m_ref, g_ref, b_ref, rw_ref, rb_ref, x1_ref, xb_ref, idx_ref, wts_ref):
    acc = ALPHA * x_ref[...] + m_ref[...]
    xc = acc - jnp.mean(acc, -1, keepdims=True)
    var = jnp.mean(xc * xc, -1, keepdims=True)
    x1 = xc * lax.rsqrt(var + NORM_EPS) * g_ref[...] + b_ref[...]
    x1_ref[...] = x1
    xb = x1.astype(BF16)
    xb_ref[...] = xb
    scores = jax.nn.sigmoid(jnp.dot(xb, rw_ref[...].astype(BF16), preferred_element_type=F32))
    sel = scores + rb_ref[...]
    tm, ne = sel.shape
    lane = _iota2((tm, ne), 1).astype(F32)
    slot = _iota2((tm, TOP_K), 1)
    idx = jnp.zeros((tm, TOP_K), F32)
    wts = jnp.zeros((tm, TOP_K), F32)
    for r in range(TOP_K):
        best = jnp.max(sel, axis=-1, keepdims=True)
        arg = jnp.min(jnp.where(sel == best, lane, float(ne)), axis=-1, keepdims=True)
        pick = lane == arg
        idx = jnp.where(slot == r, arg, idx)
        wts = jnp.where(slot == r, jnp.sum(jnp.where(pick, scores, 0.0), axis=-1, keepdims=True), wts)
        sel = jnp.where(pick, -jnp.inf, sel)
    idx_ref[...] = idx.astype(jnp.int32)
    wts_ref[...] = wts / jnp.sum(wts, -1, keepdims=True) * ROUTED_SCALE


def _ln_router(x, mix_out, ln_g, ln_b, router_w, router_b, layer, *, tm):
    n_tok, d = x.shape
    row = lambda w: pl.BlockSpec((tm, w), lambda i: (i, 0))
    par = lambda *shape: pl.BlockSpec((None,) + shape, lambda i: (layer,) + (0,) * len(shape))
    return pl.pallas_call(
        _ln_router_body,
        grid=(n_tok // tm,),
        in_specs=[row(d), row(d), par(1, d), par(1, d), par(d, N_EXPERTS), par(1, N_EXPERTS)],
        out_specs=[row(d), row(d), row(TOP_K), row(TOP_K)],
        out_shape=[jax.ShapeDtypeStruct((n_tok, d), F32), jax.ShapeDtypeStruct((n_tok, d), BF16),
                   jax.ShapeDtypeStruct((n_tok, TOP_K), jnp.int32), jax.ShapeDtypeStruct((n_tok, TOP_K), F32)],
        compiler_params=pltpu.CompilerParams(dimension_semantics=("arbitrary",), vmem_limit_bytes=VMEM_LIMIT),
    )(x, mix_out, ln_g.reshape(DEPTH, 1, d), ln_b.reshape(DEPTH, 1, d), router_w, router_b.reshape(DEPTH, 1, N_EXPERTS))


def _shared_body(x_ref, wg_ref, wu_ref, wd_ref, o_ref, wgb, wub, wdb):
    @pl.when(pl.program_id(0) == 0)
    def _():
        wgb[...] = wg_ref[...].astype(BF16)
        wub[...] = wu_ref[...].astype(BF16)
        wdb[...] = wd_ref[...].astype(BF16)

    xb = x_ref[...]
    g = jnp.dot(xb, wgb[...], preferred_element_type=F32)
    u = jnp.dot(xb, wub[...], preferred_element_type=F32)
    h = (g * jax.nn.sigmoid(g)) * u
    o_ref[...] = jnp.dot(h.astype(BF16), wdb[...], preferred_element_type=F32)


def _shared_expert(xb, ws_gate, ws_up, ws_down, layer, *, tm):
    n_tok, d = xb.shape
    ds = ws_gate.shape[-1]
    return pl.pallas_call(
        _shared_body,
        grid=(n_tok // tm,),
        in_specs=[pl.BlockSpec((tm, d), lambda i: (i, 0)),
                  pl.BlockSpec((None, d, ds), lambda i: (layer, 0, 0)),
                  pl.BlockSpec((None, d, ds), lambda i: (layer, 0, 0)),
                  pl.BlockSpec((None, ds, d), lambda i: (layer, 0, 0))],
        out_specs=pl.BlockSpec((tm, d), lambda i: (i, 0)),
        out_shape=jax.ShapeDtypeStruct((n_tok, d), F32),
        scratch_shapes=[pltpu.VMEM((d, ds), BF16), pltpu.VMEM((d, ds), BF16), pltpu.VMEM((ds, d), BF16)],
        compiler_params=pltpu.CompilerParams(dimension_semantics=("arbitrary",), vmem_limit_bytes=VMEM_LIMIT),
    )(xb, ws_gate, ws_up, ws_down)


def _ple_body(x_ref, xc_ref, p_ref, wg_ref, wp_ref, o_ref, wgb, wpb):
    @pl.when(pl.program_id(1) == 0)
    def _():
        wgb[...] = wg_ref[...].astype(BF16)
        wpb[...] = wp_ref[...].astype(BF16)

    gate = jnp.dot(x_ref[...].astype(BF16), wgb[...], preferred_element_type=F32)
    emb = jnp.dot(p_ref[...].astype(BF16), wpb[...], preferred_element_type=F32)
    o_ref[...] = xc_ref[...] + jax.nn.sigmoid(gate) * emb


def _ple(x, p, ple_gate, ple_proj, layer, *, tm, tn):
    n_tok, d = x.shape
    dp = p.shape[-1]
    return pl.pallas_call(
        _ple_body,
        grid=(d // tn, n_tok // tm),
        in_specs=[pl.BlockSpec((tm, d), lambda j, i: (i, 0)),
                  pl.BlockSpec((tm, tn), lambda j, i: (i, j)),
                  pl.BlockSpec((None, tm, dp), lambda j, i: (layer, i, 0)),
                  pl.BlockSpec((None, d, tn), lambda j, i: (layer, 0, j)),
                  pl.BlockSpec((None, dp, tn), lambda j, i: (layer, 0, j))],
        out_specs=pl.BlockSpec((tm, tn), lambda j, i: (i, j)),
        out_shape=jax.ShapeDtypeStruct((n_tok, d), F32),
        scratch_shapes=[pltpu.VMEM((d, tn), BF16), pltpu.VMEM((dp, tn), BF16)],
        compiler_params=pltpu.CompilerParams(dimension_semantics=("arbitrary", "arbitrary"),
                                             vmem_limit_bytes=VMEM_LIMIT),
    )(x, x, p, ple_gate, ple_proj)


def _experts_body(be_ref, nu_ref, x_ref, gate_ref, wg_ref, wu_ref, wd_ref, y_ref, wgb, wub, wdb):
    b = pl.program_id(0)
    e = be_ref[b]
    prev = be_ref[jnp.maximum(b - 1, 0)]

    @pl.when((b == 0) | (e != prev))
    def _():
        wgb[...] = wg_ref[...].astype(BF16)
        wub[...] = wu_ref[...].astype(BF16)
        wdb[...] = wd_ref[...].astype(BF16)

    @pl.when(b < nu_ref[0])
    def _():
        xb = x_ref[...]
        g = jnp.dot(xb, wgb[...], preferred_element_type=F32)
        u = jnp.dot(xb, wub[...], preferred_element_type=F32)
        h = (g * jax.nn.sigmoid(g)) * u
        y_ref[...] = jnp.dot(h.astype(BF16), wdb[...], preferred_element_type=F32) * gate_ref[...]

    @pl.when(b >= nu_ref[0])
    def _():
        y_ref[...] = jnp.zeros_like(y_ref)


def _experts(xg, row_gate, block_expert, n_used, we_gate, we_up, we_down, layer):
    rows, d = xg.shape
    tm = EXPERT_ROWS
    n_blocks = rows // tm
    grid_spec = pltpu.PrefetchScalarGridSpec(
        num_scalar_prefetch=2,
        grid=(n_blocks,),
        in_specs=[pl.BlockSpec((tm, d), lambda b, be, nu: (b, 0)),
                  pl.BlockSpec((tm, 1), lambda b, be, nu: (b, 0)),
                  pl.BlockSpec((None, None, d, D_EXPERT), lambda b, be, nu: (layer, be[b], 0, 0)),
                  pl.BlockSpec((None, None, d, D_EXPERT), lambda b, be, nu: (layer, be[b], 0, 0)),
                  pl.BlockSpec((None, None, D_EXPERT, d), lambda b, be, nu: (layer, be[b], 0, 0))],
        out_specs=pl.BlockSpec((tm, d), lambda b, be, nu: (b, 0)),
        scratch_shapes=[pltpu.VMEM((d, D_EXPERT), BF16), pltpu.VMEM((d, D_EXPERT), BF16),
                        pltpu.VMEM((D_EXPERT, d), BF16)],
    )
    return pl.pallas_call(
        _experts_body,
        grid_spec=grid_spec,
        out_shape=jax.ShapeDtypeStruct((rows, d), F32),
        compiler_params=pltpu.CompilerParams(dimension_semantics=("arbitrary",),
                                             vmem_limit_bytes=VMEM_LIMIT),
    )(block_expert, n_used, xg, row_gate, we_gate, we_up, we_down)


def _route_plan(idx, wts, n_tok):
    tm = EXPERT_ROWS
    n_pairs = n_tok * TOP_K
    n_blocks = (n_pairs + N_EXPERTS * (tm - 1) + tm - 1) // tm
    flat_e = idx.reshape(-1).astype(jnp.int32)
    pos = jnp.arange(n_pairs, dtype=jnp.int32)
    w_bits = lax.bitcast_convert_type(wts.reshape(-1), jnp.int32)
    sorted_e, order, sorted_w = lax.sort((flat_e, pos, w_bits), num_keys=1, is_stable=True)
    is_first = jnp.concatenate([jnp.ones((1,), bool), sorted_e[1:] != sorted_e[:-1]])
    start = lax.cummax(jnp.where(is_first, pos, 0), axis=0)
    prev_start = jnp.concatenate([jnp.zeros((1,), jnp.int32), start[:-1]])
    inc = jnp.where(is_first & (pos > 0), (pos - prev_start + tm - 1) // tm * tm, 0)
    dest = jnp.cumsum(inc) + pos - start
    packed = jnp.stack([order // TOP_K, sorted_w], axis=1)
    empty = jnp.broadcast_to(jnp.array([n_tok, 0], jnp.int32), (n_blocks * tm, 2))
    rows = empty.at[dest].set(packed, unique_indices=True)
    rows_tok = rows[:, 0]
    rows_gate = lax.bitcast_convert_type(rows[:, 1], F32)
    _, pair_row = lax.sort((order, dest), num_keys=1)
    grp_end = jnp.searchsorted(sorted_e, jnp.arange(N_EXPERTS, dtype=jnp.int32), side='right').astype(jnp.int32)
    counts = grp_end - jnp.concatenate([jnp.zeros((1,), jnp.int32), grp_end[:-1]])
    pad_end = jnp.cumsum((counts + tm - 1) // tm * tm)
    n_used = (pad_end[-1] // tm).astype(jnp.int32)
    blk = jnp.arange(n_blocks, dtype=jnp.int32)
    be = jnp.minimum(jnp.searchsorted(pad_end, blk * tm, side='right'), N_EXPERTS - 1).astype(jnp.int32)
    be = jnp.where(blk < n_used, be, be[jnp.maximum(n_used - 1, 0)])
    return rows_tok, rows_gate.reshape(-1, 1), pair_row.reshape(n_tok, TOP_K), be, n_used.reshape(1)


def _combine_body(cur_ref, nxt_ref, y_hbm, sh_ref, x_ref, g_ref, b_ref, o_ref, buf, sem, *, tm):
    i = pl.program_id(0)
    n = pl.num_programs(0)
    slot = i % 2
    n_rows = TOP_K * tm

    def row_copy(src_row, dst_slot, r):
        return pltpu.make_async_copy(y_hbm.at[pl.ds(src_row, 1), :], buf.at[dst_slot, pl.ds(r, 1), :],
                                     sem.at[dst_slot])

    def issue(idx_ref, dst_slot):
        def body(r, carry):
            row_copy(idx_ref[0, r], dst_slot, r).start()
            return carry
        lax.fori_loop(0, n_rows, body, 0, unroll=8)

    @pl.when(i == 0)
    def _():
        issue(cur_ref, 0)

    @pl.when(i + 1 < n)
    def _():
        issue(nxt_ref, 1 - slot)

    def wait_body(r, carry):
        row_copy(0, slot, r).wait()
        return carry
    lax.fori_loop(0, n_rows, wait_body, 0, unroll=8)

    acc = ALPHA * x_ref[...] + sh_ref[...]
    for kk in range(TOP_K):
        acc = acc + buf[slot, kk * tm:(kk + 1) * tm, :]
    xc = acc - jnp.mean(acc, -1, keepdims=True)
    var = jnp.mean(xc * xc, -1, keepdims=True)
    o_ref[...] = xc * lax.rsqrt(var + NORM_EPS) * g_ref[...] + b_ref[...]


def _combine_ln(y_rows, pair_row, shared, x, ln_g, ln_b, layer, *, tm):
    n_tok, d = x.shape
    n_steps = n_tok // tm
    idx = pair_row.reshape(n_steps, tm, TOP_K).transpose(0, 2, 1).reshape(n_steps, 1, TOP_K * tm)
    idx_spec = lambda f: pl.BlockSpec((None, 1, TOP_K * tm), f, memory_space=pltpu.SMEM)
    return pl.pallas_call(
        functools.partial(_combine_body, tm=tm),
        grid=(n_steps,),
        in_specs=[idx_spec(lambda i: (i, 0, 0)),
                  idx_spec(lambda i: (jnp.minimum(i + 1, n_steps - 1), 0, 0)),
                  pl.BlockSpec(memory_space=pl.ANY),
                  pl.BlockSpec((tm, d), lambda i: (i, 0)),
                  pl.BlockSpec((tm, d), lambda i: (i, 0)),
                  pl.BlockSpec((None, 1, d), lambda i: (layer, 0, 0)),
                  pl.BlockSpec((None, 1, d), lambda i: (layer, 0, 0))],
        out_specs=pl.BlockSpec((tm, d), lambda i: (i, 0)),
        out_shape=jax.ShapeDtypeStruct((n_tok, d), F32),
        scratch_shapes=[pltpu.VMEM((2, TOP_K * tm, d), F32), pltpu.SemaphoreType.DMA((2,))],
        compiler_params=pltpu.CompilerParams(dimension_semantics=("arbitrary",), vmem_limit_bytes=VMEM_LIMIT),
    )(idx, idx, y_rows, shared, x, ln_g.reshape(DEPTH, 1, d), ln_b.reshape(DEPTH, 1, d))


CH = 64


def _dot(a, b):
    return jnp.dot(a.astype(BF16), b.astype(BF16), preferred_element_type=F32)


def _dot_nt(a, b):
    return lax.dot_general(a.astype(BF16), b.astype(BF16), (((1,), (1,)), ((), ())), preferred_element_type=F32)


def _dot_tn(a, b):
    return lax.dot_general(a.astype(BF16), b.astype(BF16), (((0,), (0,)), ((), ())), preferred_element_type=F32)


def _split2(a):
    hi = a.astype(BF16)
    lo = (a - hi.astype(F32)).astype(BF16)
    return hi, lo


def _dot_sel(a, sel):
    hi = a.astype(BF16)
    r = a - hi.astype(F32)
    mid = r.astype(BF16)
    lo = (r - mid.astype(F32)).astype(BF16)
    d = lambda x: jnp.dot(x, sel, preferred_element_type=F32)
    return d(hi) + d(mid) + d(lo)


def _dot_hi(a, b):
    ah, al = _split2(a)
    bh, bl = _split2(b)
    d = lambda x, y: jnp.dot(x, y, preferred_element_type=F32)
    return d(ah, bh) + d(ah, bl) + d(al, bh)


def _iota2(shape, dim):
    return lax.broadcasted_iota(jnp.int32, shape, dim)


def _decay_terms(g_row, c):
    ii = _iota2((c, c), 0)
    mm = _iota2((c, c), 1)
    l_incl = jnp.where(ii >= mm, g_row, 0.0)
    l_excl = jnp.where(mm > ii, g_row, 0.0)
    m2 = _iota2((c, 256), 0)
    j2 = _iota2((c, 256), 1)
    rhs1 = jnp.where(j2 >= 128, 1.0, jnp.where(m2 > j2, 1.0, 0.0)).astype(BF16)
    ones = jnp.ones((c, 128), BF16)
    out1 = _dot_sel(l_incl, rhs1)
    return out1[:, :c], out1[:, 128:], _dot_sel(l_excl, ones)


def _col_bcast(row, c):
    ii = _iota2((c, c), 0)
    mm = _iota2((c, c), 1)
    return _dot_sel(jnp.where(ii == mm, row, 0.0), jnp.ones((c, 128), BF16))


def _softplus(x):
    return jnp.maximum(x, 0.0) + jnp.log1p(jnp.exp(-jnp.abs(x)))


def _silu(x):
    return x * jax.nn.sigmoid(x)


def _conv_block(x_ref, e_ref, w_ref, rows):
    e_ref[8:rows + 8, :] = x_ref[...]
    w = w_ref[...]
    return (e_ref[8:rows + 8, :] * w[3:4, :] + e_ref[7:rows + 7, :] * w[2:3, :]
            + e_ref[6:rows + 6, :] * w[1:2, :] + e_ref[5:rows + 5, :] * w[0:1, :])


def _conv_carry(e_ref, rows):
    e_ref[0:8, :] = e_ref[rows:rows + 8, :]


def _ret_body(lg_ref, cd_ref, mix_ref, q_ref, k_ref, v_ref, g_ref, cos_ref, sin_ref, o_ref, s_ref, *, rows, dk):
    del mix_ref
    h = pl.program_id(1)
    lg = lg_ref[h]

    @pl.when(pl.program_id(2) == 0)
    def _():
        s_ref[...] = jnp.zeros_like(s_ref)

    cos = cos_ref[...]
    sin = sin_ref[...]
    half = dk // 2

    def rot(x):
        x1 = x[:, :half]
        x2 = x[:, half:]
        return jnp.concatenate([x1 * cos - x2 * sin, x1 * sin + x2 * cos], axis=-1)

    q = rot(q_ref[...])
    k = rot(k_ref[...]) * (dk ** -0.5)
    v = v_ref[...]
    c = CH
    ii = _iota2((c, c), 0)
    jj = _iota2((c, c), 1)
    dmask = jnp.where(ii >= jj, jnp.exp(lg * jnp.maximum(ii - jj, 0).astype(F32)), 0.0)
    ti = _iota2((c, 1), 0).astype(F32)
    q_dec = jnp.exp((ti + 1.0) * lg)
    k_dec = jnp.exp((c - 1.0 - ti) * lg)
    s = s_ref[...]
    for ch in range(rows // c):
        sl = slice(ch * c, (ch + 1) * c)
        qc, kc, vc = q[sl], k[sl], v[sl]
        att = _dot_nt(qc, kc) * dmask
        o = _dot(att, vc) + _dot(qc * q_dec, s)
        s = s * cd_ref[h] + _dot_tn(kc * k_dec, vc)
        oc = o - jnp.mean(o, -1, keepdims=True)
        on = oc * lax.rsqrt(jnp.mean(oc * oc, -1, keepdims=True) + NORM_EPS)
        o_ref[sl, :] = _silu(g_ref[sl, :]) * on
    s_ref[...] = s


def _retention_prompt(mix, proj, cos, sin, log_gamma, *, rows=256):
    ntb = SEQ // rows
    heads, dk = RET_HEADS, RET_DK
    cdec = jnp.exp(CH * log_gamma)
    smem = pl.BlockSpec(memory_space=pltpu.SMEM)

    def blk(part):
        return pl.BlockSpec((rows, dk), lambda b, h, t: (b * ntb + t, part * heads + h))

    return pl.pallas_call(
        functools.partial(_ret_body, rows=rows, dk=dk),
        grid=(BATCH, heads, ntb),
        in_specs=[smem, smem, pl.BlockSpec(memory_space=pl.ANY), blk(0), blk(1), blk(2), blk(3),
                  pl.BlockSpec((rows, dk // 2), lambda b, h, t: (t, 0)),
                  pl.BlockSpec((rows, dk // 2), lambda b, h, t: (t, 0))],
        out_specs=[pl.BlockSpec((rows, dk), lambda b, h, t: (b * ntb + t, h)),
                   pl.BlockSpec((None, None, dk, dk), lambda b, h, t: (b, h, 0, 0))],
        out_shape=[jax.ShapeDtypeStruct(mix.shape, F32),
                   jax.ShapeDtypeStruct((BATCH, heads, dk, dk), F32)],
        input_output_aliases={2: 0},
        compiler_params=pltpu.CompilerParams(dimension_semantics=("arbitrary",) * 3, vmem_limit_bytes=VMEM_LIMIT),
    )(log_gamma, cdec, mix, proj, proj, proj, proj, cos, sin)


def _ssd_body(alog_ref, dtb_ref, dsk_ref, mix_ref, x_ref, bm_ref, cm_ref, z_ref, dt_ref,
              wx_ref, wb_ref, wc_ref, bx_ref, bb_ref, bc_ref, nw_ref,
              o_ref, st_ref, cx_ref, cb_ref, cc_ref, ex, eb, ec, s_t, *, rows, hpg, hd):
    del mix_ref
    g = pl.program_id(1)
    tb = pl.program_id(2)
    last_tb = pl.num_programs(2) - 1
    c = CH
    nch = rows // c

    @pl.when(tb == 0)
    def _():
        s_t[...] = jnp.zeros_like(s_t)
        ex[0:8, :] = jnp.zeros((8, ex.shape[1]), F32)
        eb[0:8, :] = jnp.zeros((8, eb.shape[1]), F32)
        ec[0:8, :] = jnp.zeros((8, ec.shape[1]), F32)

    xs = _silu(_conv_block(x_ref, ex, wx_ref, rows) + bx_ref[...])
    bm = _silu(_conv_block(bm_ref, eb, wb_ref, rows) + bb_ref[...])
    cm = _silu(_conv_block(cm_ref, ec, wc_ref, rows) + bc_ref[...])

    @pl.when(tb == last_tb)
    def _():
        cx_ref[...] = ex[rows + 5:rows + 8, :]
        cb_ref[...] = eb[rows + 5:rows + 8, :]
        cc_ref[...] = ec[rows + 5:rows + 8, :]

    _conv_carry(ex, rows)
    _conv_carry(eb, rows)
    _conv_carry(ec, rows)

    ii = _iota2((c, c), 0)
    jj = _iota2((c, c), 1)
    incl = ii >= jj
    nw = nw_ref[...]
    for ch in range(nch):
        sl = slice(ch * c, (ch + 1) * c)
        xc = xs[sl]
        bc = bm[sl]
        cc = cm[sl]
        cbm = _dot_nt(cc, bc)
        st = s_t[...]
        ystate = _dot(cc, st)
        ys, wxs, decs = [], [], []
        for r in range(hpg):
            h = g * hpg + r
            dt_row = _softplus(dt_ref[r, ch] + dtb_ref[h])
            a_vec = -jnp.exp(jnp.zeros((1, c), F32) + alog_ref[h])
            dmat, cs_cb, rest_cb = _decay_terms(dt_row * a_vec, c)
            dt_cb = _col_bcast(dt_row, c)
            lm = jnp.where(incl, jnp.exp(jnp.where(incl, dmat, 0.0)), 0.0)
            x_r = xc[:, r * hd:(r + 1) * hd]
            y_r = _dot(cbm * lm * dt_row, x_r) + ystate[:, r * hd:(r + 1) * hd] * jnp.exp(cs_cb[:, :hd])
            ys.append(y_r + dsk_ref[h] * x_r)
            wxs.append(x_r * (jnp.exp(rest_cb[:, :hd]) * dt_cb[:, :hd]))
            decs.append(jnp.exp(cs_cb[c - 1:c, :hd]))
        s_t[...] = st * jnp.concatenate(decs, axis=-1) + _dot_tn(bc, jnp.concatenate(wxs, axis=-1))
        y = jnp.concatenate(ys, axis=-1) * _silu(z_ref[sl, :])
        o_ref[sl, :] = y * lax.rsqrt(jnp.mean(y * y, -1, keepdims=True) + NORM_EPS) * nw

    @pl.when(tb == last_tb)
    def _():
        st_ref[...] = s_t[...].T


def _ssd_prompt(mix, proj, dt_t, conv_w, conv_b, a_log, dt_bias, d_skip, norm_w, *, rows=128):
    ntb = SEQ // rows
    nch = rows // CH
    groups, hpg, hd, n_state = SSM_GROUPS, SSM_HEADS // SSM_GROUPS, SSM_P, SSM_N
    gw = hpg * hd
    z_col = (4 * RET_HEADS * RET_DK) // gw
    x_col = z_col + SSM_INNER // gw
    b_col = (x_col * gw + SSM_INNER) // n_state
    c_col = b_col + groups
    wb_col = SSM_INNER // n_state
    wc_col = wb_col + groups
    out_col = (RET_HEADS * RET_DV) // gw
    smem = pl.BlockSpec(memory_space=pltpu.SMEM)
    in_specs = [
        smem, smem, smem,
        pl.BlockSpec(memory_space=pl.ANY),
        pl.BlockSpec((rows, gw), lambda b, g, t: (b * ntb + t, x_col + g)),
        pl.BlockSpec((rows, n_state), lambda b, g, t: (b * ntb + t, b_col + g)),
        pl.BlockSpec((rows, n_state), lambda b, g, t: (b * ntb + t, c_col + g)),
        pl.BlockSpec((rows, gw), lambda b, g, t: (b * ntb + t, z_col + g)),
        pl.BlockSpec((hpg, nch, 1, CH), lambda b, g, t: (g, b * ntb + t, 0, 0)),
        pl.BlockSpec((CONV_W, gw), lambda b, g, t: (0, g)),
        pl.BlockSpec((CONV_W, n_state), lambda b, g, t: (0, wb_col + g)),
        pl.BlockSpec((CONV_W, n_state), lambda b, g, t: (0, wc_col + g)),
        pl.BlockSpec((1, gw), lambda b, g, t: (0, g)),
        pl.BlockSpec((1, n_state), lambda b, g, t: (0, wb_col + g)),
        pl.BlockSpec((1, n_state), lambda b, g, t: (0, wc_col + g)),
        pl.BlockSpec((1, gw), lambda b, g, t: (0, g)),
    ]
    out_specs = [
        pl.BlockSpec((rows, gw), lambda b, g, t: (b * ntb + t, out_col + g)),
        pl.BlockSpec((None, gw, n_state), lambda b, g, t: (b, g, 0)),
        pl.BlockSpec((None, CONV_W - 1, gw), lambda b, g, t: (b, 0, g)),
        pl.BlockSpec((None, CONV_W - 1, n_state), lambda b, g, t: (b, 0, g)),
        pl.BlockSpec((None, CONV_W - 1, n_state), lambda b, g, t: (b, 0, g)),
    ]
    out_shape = [
        jax.ShapeDtypeStruct(mix.shape, F32),
        jax.ShapeDtypeStruct((BATCH, SSM_INNER, n_state), F32),
        jax.ShapeDtypeStruct((BATCH, CONV_W - 1, SSM_INNER), F32),
        jax.ShapeDtypeStruct((BATCH, CONV_W - 1, groups * n_state), F32),
        jax.ShapeDtypeStruct((BATCH, CONV_W - 1, groups * n_state), F32),
    ]
    scratch = [pltpu.VMEM((rows + 8, gw), F32), pltpu.VMEM((rows + 8, n_state), F32),
               pltpu.VMEM((rows + 8, n_state), F32), pltpu.VMEM((n_state, gw), F32)]
    return pl.pallas_call(
        functools.partial(_ssd_body, rows=rows, hpg=hpg, hd=hd),
        grid=(BATCH, groups, ntb),
        in_specs=in_specs, out_specs=out_specs, out_shape=out_shape, scratch_shapes=scratch,
        input_output_aliases={3: 0},
        compiler_params=pltpu.CompilerParams(dimension_semantics=("arbitrary",) * 3, vmem_limit_bytes=VMEM_LIMIT),
    )(a_log, dt_bias, d_skip, mix, proj, proj, proj, proj, dt_t, conv_w, conv_w, conv_w,
      conv_b, conv_b, conv_b, norm_w)


def _inv_unit_lower(a, c):
    ii = _iota2((c, c), 0)
    jj = _iota2((c, c), 1)
    n = -a
    p = jnp.where(ii == jj, 1.0, 0.0) + n
    for _ in range(c.bit_length() - 2):
        n = _dot_hi(n, n)
        p = p + _dot_hi(p, n)
    return p


def _gdn_body(alog_ref, dtb_ref, mix_ref, q_ref, k_ref, v_ref, z_ref, a_ref, b_ref, wq_ref, wk_ref, wv_ref, nw_ref,
              o_ref, s_ref, cq_ref, ck_ref, cv_ref, eq, ek, ev, *, rows, rep, dk, dv):
    del mix_ref
    hk = pl.program_id(1)
    tb = pl.program_id(2)
    last_tb = pl.num_programs(2) - 1
    c = CH
    nch = rows // c

    @pl.when(tb == 0)
    def _():
        s_ref[...] = jnp.zeros_like(s_ref)
        eq[0:8, :] = jnp.zeros((8, eq.shape[1]), F32)
        ek[0:8, :] = jnp.zeros((8, ek.shape[1]), F32)
        ev[0:8, :] = jnp.zeros((8, ev.shape[1]), F32)

    q = _silu(_conv_block(q_ref, eq, wq_ref, rows))
    k = _silu(_conv_block(k_ref, ek, wk_ref, rows))
    v = _silu(_conv_block(v_ref, ev, wv_ref, rows))

    @pl.when(tb == last_tb)
    def _():
        cq_ref[...] = eq[rows + 5:rows + 8, :]
        ck_ref[...] = ek[rows + 5:rows + 8, :]
        cv_ref[...] = ev[rows + 5:rows + 8, :]

    _conv_carry(eq, rows)
    _conv_carry(ek, rows)
    _conv_carry(ev, rows)

    q = q * lax.rsqrt(jnp.sum(q * q, -1, keepdims=True) + 1e-6) * (dk ** -0.5)
    k = k * lax.rsqrt(jnp.sum(k * k, -1, keepdims=True) + 1e-6)

    c2 = 2 * c
    ii = _iota2((c2, c2), 0)
    jj = _iota2((c2, c2), 1)
    same = (ii >= c) == (jj >= c)
    ti = ii & (c - 1)
    tj = jj & (c - 1)
    incl = same & (ti >= tj)
    strict = same & (ti > tj)
    later = same & (tj > ti)
    diag = ii == jj
    top = ii < c
    col2 = _iota2((c2, 2 * c2), 1)
    row2 = _iota2((c2, 2 * c2), 0)
    sel1 = jnp.where(col2 >= c2, 1.0, jnp.where(((row2 >= c) == (col2 >= c)) & ((row2 & (c - 1)) > (col2 & (c - 1))),
                                                 1.0, 0.0)).astype(BF16)
    sel2 = jnp.where((_iota2((2 * c2, 2 * c2), 0) >= c2) == (_iota2((2 * c2, 2 * c2), 1) >= c2), 1.0, 0.0).astype(BF16)
    eye = jnp.where(diag, 1.0, 0.0)
    lane = _iota2((1, c2), 1)
    alog = jnp.where(lane < c, alog_ref[hk * 2], alog_ref[hk * 2 + 1])
    dtb = jnp.where(lane < c, dtb_ref[hk * 2], dtb_ref[hk * 2 + 1])
    neg_a = -jnp.exp(alog)
    nw = nw_ref[...]

    pre = []
    for ch in range(nch):
        sl = slice(ch * c, (ch + 1) * c)
        q2 = jnp.concatenate([q[sl], q[sl]], axis=0)
        k2 = jnp.concatenate([k[sl], k[sl]], axis=0)
        v2 = jnp.concatenate([v[sl, :dv], v[sl, dv:]], axis=0)
        g_row = neg_a * _softplus(a_ref[ch] + dtb)
        beta_row = jax.nn.sigmoid(b_ref[ch])
        out1 = _dot_sel(jnp.where(incl, g_row, 0.0), sel1)
        out2 = _dot_sel(jnp.concatenate([jnp.where(later, g_row, 0.0), jnp.where(diag, beta_row, 0.0)], axis=1), sel2)
        gcum_cb, rest_cb, beta_cb = out1[:, c2:], out2[:, :c2], out2[:, c2:]
        decay = jnp.where(incl, jnp.exp(jnp.where(incl, out1[:, :c2], 0.0)), 0.0)
        eg = jnp.exp(gcum_cb)
        kb2 = k2 * beta_cb
        pre.append(dict(
            n=-jnp.where(strict, _dot_nt(kb2, k2) * decay, 0.0),
            rhs=jnp.concatenate([v2 * beta_cb, kb2 * eg], axis=1),
            qe=q2 * eg,
            qk=_dot_nt(q2, k2) * decay,
            kw=k2 * jnp.exp(rest_cb),
            declast=jnp.concatenate([eg[c - 1:c, :], eg[c2 - 1:c2, :]], axis=1)))

    ps = [eye + d["n"] for d in pre]
    ms = [_dot_hi(d["n"], d["n"]) for d in pre]
    n_steps = c.bit_length() - 2
    for step in range(n_steps):
        if step < n_steps - 1:
            rs = [_dot_hi(jnp.concatenate([p, m], axis=0), m) for p, m in zip(ps, ms)]
            ps = [p + r[:c2] for p, r in zip(ps, rs)]
            ms = [r[c2:] for r in rs]
        else:
            ps = [p + _dot_hi(p, m) for p, m in zip(ps, ms)]
    sols = [_dot_hi(p, d["rhs"]) for p, d in zip(ps, pre)]

    s_cat = jnp.concatenate([s_ref[0], s_ref[1]], axis=1)
    for ch in range(nch):
        sl = slice(ch * c, (ch + 1) * c)
        d = pre[ch]
        sol = sols[ch]
        r = _dot(jnp.concatenate([sol[:, dv:], d["qe"]], axis=0), s_cat)
        u = sol[:, :dv] - jnp.where(top, r[:c2, :dv], r[:c2, dv:])
        o = jnp.where(top, r[c2:, :dv], r[c2:, dv:]) + _dot(d["qk"], u)
        u_bd = jnp.concatenate([jnp.where(top, u, 0.0), jnp.where(top, 0.0, u)], axis=1)
        s_cat = s_cat * d["declast"] + _dot_tn(d["kw"], u_bd)
        zz = jnp.concatenate([z_ref[sl, :dv], z_ref[sl, dv:]], axis=0)
        res = o * lax.rsqrt(jnp.mean(o * o, -1, keepdims=True) + NORM_EPS) * nw * _silu(zz)
        o_ref[sl, :dv] = res[:c]
        o_ref[sl, dv:] = res[c:]
    s_ref[0] = s_cat[:, :dv]
    s_ref[1] = s_cat[:, dv:]


def _gdn_prompt(mix, proj, a_t, b_t, conv_w, a_log, dt_bias, norm_w, *, rows=256):
    hk, hv, dk, dv = GDN_HK, GDN_HV, GDN_DK, GDN_DV
    rep = hv // hk
    assert rep == 2 and dk == dv
    ntb = SEQ // rows
    nch = rows // CH
    vw = rep * dv
    kcol = hk
    vcol = (2 * GDN_QK) // vw
    zcol = (2 * GDN_QK + GDN_VW) // vw
    smem = pl.BlockSpec(memory_space=pltpu.SMEM)
    in_specs = [
        smem, smem,
        pl.BlockSpec(memory_space=pl.ANY),
        pl.BlockSpec((rows, dk), lambda b, h, t: (b * ntb + t, h)),
        pl.BlockSpec((rows, dk), lambda b, h, t: (b * ntb + t, kcol + h)),
        pl.BlockSpec((rows, vw), lambda b, h, t: (b * ntb + t, vcol + h)),
        pl.BlockSpec((rows, vw), lambda b, h, t: (b * ntb + t, zcol + h)),
        pl.BlockSpec((None, nch, 1, rep * CH), lambda b, h, t: (h, b * ntb + t, 0, 0)),
        pl.BlockSpec((None, nch, 1, rep * CH), lambda b, h, t: (h, b * ntb + t, 0, 0)),
        pl.BlockSpec((CONV_W, dk), lambda b, h, t: (0, h)),
        pl.BlockSpec((CONV_W, dk), lambda b, h, t: (0, kcol + h)),
        pl.BlockSpec((CONV_W, vw), lambda b, h, t: (0, vcol + h)),
        pl.BlockSpec((1, dv), lambda b, h, t: (0, 0)),
    ]
    out_specs = [
        pl.BlockSpec((rows, vw), lambda b, h, t: (b * ntb + t, h)),
        pl.BlockSpec((None, rep, dk, dv), lambda b, h, t: (b, h, 0, 0)),
        pl.BlockSpec((None, CONV_W - 1, dk), lambda b, h, t: (b, 0, h)),
        pl.BlockSpec((None, CONV_W - 1, dk), lambda b, h, t: (b, 0, h)),
        pl.BlockSpec((None, CONV_W - 1, vw), lambda b, h, t: (b, 0, h)),
    ]
    out_shape = [
        jax.ShapeDtypeStruct(mix.shape, F32),
        jax.ShapeDtypeStruct((BATCH, hv, dk, dv), F32),
        jax.ShapeDtypeStruct((BATCH, CONV_W - 1, GDN_QK), F32),
        jax.ShapeDtypeStruct((BATCH, CONV_W - 1, GDN_QK), F32),
        jax.ShapeDtypeStruct((BATCH, CONV_W - 1, GDN_VW), F32),
    ]
    scratch = [pltpu.VMEM((rows + 8, dk), F32), pltpu.VMEM((rows + 8, dk), F32), pltpu.VMEM((rows + 8, vw), F32)]
    return pl.pallas_call(
        functools.partial(_gdn_body, rows=rows, rep=rep, dk=dk, dv=dv),
        grid=(BATCH, hk, ntb),
        in_specs=in_specs, out_specs=out_specs, out_shape=out_shape, scratch_shapes=scratch,
        input_output_aliases={2: 0},
        compiler_params=pltpu.CompilerParams(dimension_semantics=("arbitrary",) * 3, vmem_limit_bytes=VMEM_LIMIT),
    )(a_log, dt_bias, mix, proj, proj, proj, proj, a_t, b_t, conv_w, conv_w, conv_w, norm_w)


def _layer_norm(xf, g, b):
    xc = xf - jnp.mean(xf, -1, keepdims=True)
    var = jnp.mean(xc * xc, -1, keepdims=True)
    return xc * lax.rsqrt(var + NORM_EPS) * g + b


def _rms(xf):
    return xf * lax.rsqrt(jnp.mean(xf * xf, -1, keepdims=True) + NORM_EPS)


def _head_ln(xf):
    xc = xf - jnp.mean(xf, -1, keepdims=True)
    return xc * lax.rsqrt(jnp.mean(xc * xc, -1, keepdims=True) + NORM_EPS)


def _l2norm(xf):
    return xf * lax.rsqrt(jnp.sum(xf * xf, -1, keepdims=True) + 1e-6)


def _rotary(t, pos):
    half = t.shape[-1] // 2
    inv = ROPE_BASE ** (-jnp.arange(half, dtype=F32) / half)
    ang = pos.astype(F32)[:, None] * inv
    cos = jnp.cos(ang)[None, :, None, :]
    sin = jnp.sin(ang)[None, :, None, :]
    t1, t2 = t[..., :half], t[..., half:]
    return jnp.concatenate([t1 * cos - t2 * sin, t1 * sin + t2 * cos], -1)


def _chunk_len(L):
    return CHUNK if L % CHUNK == 0 else L


def _to_chunks(t, c):
    return t.reshape(t.shape[0], t.shape[1] // c, c, *t.shape[2:]).swapaxes(0, 1)


def _from_chunks(t):
    t = t.swapaxes(0, 1)
    return t.reshape(t.shape[0], t.shape[1] * t.shape[2], *t.shape[3:])


def _causal_conv(x, prev, w, b):
    L = x.shape[1]
    xp = jnp.concatenate([prev.astype(x.dtype), x], axis=1)
    y = xp[:, CONV_W - 1:] * w[CONV_W - 1]
    for i in range(CONV_W - 1):
        y = y + xp[:, i:i + L] * w[i]
    if b is not None:
        y = y + b
    return y, xp[:, L:]


def _retention_scan(q, k, v, s0, log_gamma):
    L = q.shape[1]
    c = _chunk_len(L)
    idx = jnp.arange(c, dtype=F32)
    diff = idx[:, None] - idx[None, :]
    dmask = jnp.exp(jnp.where((diff >= 0)[None], log_gamma[:, None, None] * diff[None], -jnp.inf))
    q_dec = jnp.exp((idx[:, None] + 1.0) * log_gamma)[:, :, None]
    k_dec = jnp.exp((c - 1.0 - idx)[:, None] * log_gamma)[:, :, None]
    c_dec = jnp.exp(c * log_gamma)[:, None, None]

    def step(s, inp):
        qc, kc, vc = inp
        att = jnp.einsum('bihd,bjhd->bhij', qc, kc) * dmask
        o = jnp.einsum('bhij,bjhe->bihe', att, vc) + jnp.einsum('bihd,bhde->bihe', qc * q_dec, s)
        s = s * c_dec + jnp.einsum('bjhd,bjhe->bhde', kc * k_dec, vc)
        return s, o

    s, o = lax.scan(step, s0, (_to_chunks(q, c), _to_chunks(k, c), _to_chunks(v, c)))
    return _from_chunks(o), s


def _ssd_scan(x, dt, a, bm, cm, s0):
    L = x.shape[1]
    c = _chunk_len(L)
    ar = jnp.arange(c)
    mask5 = (ar[:, None] >= ar[None, :])[None, :, :, None, None]

    def step(s, inp):
        xc, dtc, bc, cc = inp
        cs = jnp.cumsum(dtc * a, axis=1)
        lmat = jnp.exp(jnp.where(mask5, cs[:, :, None] - cs[:, None, :], -jnp.inf))
        cb = jnp.einsum('bign,bjgn->bijg', cc, bc)
        y = jnp.einsum('bijg,bijgr,bjgr,bjgrp->bigrp', cb, lmat, dtc, xc)
        y = y + jnp.einsum('bign,bgrpn->bigrp', cc, s) * jnp.exp(cs)[..., None]
        last = cs[:, -1]
        w_end = jnp.exp(last[:, None] - cs) * dtc
        s = s * jnp.exp(last)[..., None, None] + jnp.einsum('bjgr,bjgn,bjgrp->bgrpn', w_end, bc, xc)
        return s, y

    s, y = lax.scan(step, s0, (_to_chunks(x, c), _to_chunks(dt, c), _to_chunks(bm, c), _to_chunks(cm, c)))
    return _from_chunks(y), s


def _gdn_scan(q, k, v, g, beta, s0):
    L = q.shape[1]
    dv = v.shape[-1]
    c = _chunk_len(L)
    ar = jnp.arange(c)
    incl = ar[:, None] >= ar[None, :]
    strict = ar[:, None] > ar[None, :]
    eye = jnp.eye(c, dtype=F32)

    def step(s, inp):
        qc, kc, vc, gc, bc = inp
        gcum = jnp.cumsum(gc, axis=1)
        gh = gcum.swapaxes(1, 2)
        decay = jnp.exp(jnp.where(incl, gh[..., :, None] - gh[..., None, :], -jnp.inf))
        kb = kc * bc[..., None]
        a_mat = jnp.where(strict, jnp.einsum('bihd,bjhd->bhij', kb, kc) * decay, 0.0)
        rhs = jnp.concatenate([(vc * bc[..., None]).swapaxes(1, 2),
                               (kb * jnp.exp(gcum)[..., None]).swapaxes(1, 2)], -1)
        sol = lax.linalg.triangular_solve(a_mat + eye, rhs, left_side=True, lower=True, unit_diagonal=True)
        u = sol[..., :dv] - jnp.einsum('bhid,bhde->bhie', sol[..., dv:], s)
        qk = jnp.einsum('bihd,bjhd->bhij', qc, kc) * decay
        o = jnp.einsum('bihd,bhde->bihe', qc * jnp.exp(gcum)[..., None], s) + jnp.einsum('bhij,bhje->bihe', qk, u)
        g_last = gcum[:, -1]
        s = s * jnp.exp(g_last)[..., None, None] + jnp.einsum(
            'bjhd,bhje->bhde', kc * jnp.exp(g_last[:, None] - gcum)[..., None], u)
        return s, o

    xs = (_to_chunks(q, c), _to_chunks(k, c), _to_chunks(v, c), _to_chunks(g, c), _to_chunks(beta, c))
    s, o = lax.scan(step, s0, xs)
    return _from_chunks(o), s


def _even_core(proj, dt_raw, pos, s_ret, s_ssm, conv_prev, conv_w, conv_b, dt_bias, a_log, d_skip, norm_w):
    bsz, L, _ = proj.shape
    q, k, v, g, z, xbc = _split(proj, EVEN_SPLITS[:-1])
    q = _rotary(q.reshape(bsz, L, RET_HEADS, RET_DK), pos)
    k = _rotary(k.reshape(bsz, L, RET_HEADS, RET_DK), pos) * RET_DK ** -0.5
    v = v.reshape(bsz, L, RET_HEADS, RET_DV)
    log_gamma = jnp.log1p(-jnp.exp2(-5.0 - jnp.arange(RET_HEADS, dtype=F32)))
    o_ret, s_ret_new = _retention_scan(q, k, v, s_ret, log_gamma)
    o_ret = jax.nn.silu(g) * _head_ln(o_ret).reshape(bsz, L, -1)
    xbc, conv_new = _causal_conv(xbc, conv_prev, conv_w, conv_b)
    xs, bm, cm = _split(jax.nn.silu(xbc), (SSM_INNER, SSM_GROUPS * SSM_N, SSM_GROUPS * SSM_N))
    r = SSM_HEADS // SSM_GROUPS
    xs = xs.reshape(bsz, L, SSM_GROUPS, r, SSM_P)
    dt = jax.nn.softplus(dt_raw + dt_bias).reshape(bsz, L, SSM_GROUPS, r)
    a = -jnp.exp(a_log).reshape(SSM_GROUPS, r)
    y, s_ssm_new = _ssd_scan(xs, dt, a, bm.reshape(bsz, L, SSM_GROUPS, SSM_N), cm.reshape(bsz, L, SSM_GROUPS, SSM_N),
                             s_ssm.reshape(bsz, SSM_GROUPS, r, SSM_P, SSM_N))
    y = y + d_skip.reshape(SSM_GROUPS, r, 1) * xs
    y = y.reshape(bsz, L, SSM_INNER) * jax.nn.silu(z)
    y = _rms(y.reshape(bsz, L, SSM_GROUPS, -1)).reshape(bsz, L, SSM_INNER) * norm_w
    mix = jnp.concatenate([o_ret, y], -1)
    return mix, s_ret_new, s_ssm_new.reshape(bsz, SSM_HEADS, SSM_P, SSM_N), conv_new


def _odd_core(proj, ab, s_gdn, conv_prev, conv_w, dt_bias, a_log, norm_w):
    bsz, L, _ = proj.shape
    qkv, z = _split(proj, ODD_SPLITS[:2])
    a, b = _split(ab, ODD_SPLITS[2:])
    qkv, conv_new = _causal_conv(qkv, conv_prev, conv_w, None)
    q, k, v = _split(jax.nn.silu(qkv), (GDN_QK, GDN_QK, GDN_VW))
    rep = GDN_HV // GDN_HK
    q = jnp.repeat(_l2norm(q.reshape(bsz, L, GDN_HK, GDN_DK)), rep, axis=2) * GDN_DK ** -0.5
    k = jnp.repeat(_l2norm(k.reshape(bsz, L, GDN_HK, GDN_DK)), rep, axis=2)
    v = v.reshape(bsz, L, GDN_HV, GDN_DV)
    g = -jnp.exp(a_log) * jax.nn.softplus(a + dt_bias)
    beta = jax.nn.sigmoid(b)
    o, s_new = _gdn_scan(q, k, v, g, beta, s_gdn)
    o = _rms(o) * norm_w * jax.nn.silu(z.reshape(bsz, L, GDN_HV, GDN_DV))
    return o.reshape(bsz, L, GDN_VW), s_new, conv_new


def _post_block(x, mix_out, p, layer, ln1_g, ln1_b, ln2_g, ln2_b, router_w, router_b, we_gate, we_up, we_down,
                ws_gate, ws_up, ws_down, ple_proj, ple_gate):
    n_tok = x.shape[0]
    x, xb, idx, wts = _ln_router(x, mix_out, ln1_g, ln1_b, router_w, router_b, layer, tm=520)
    shared = _shared_expert(xb, ws_gate, ws_up, ws_down, layer, tm=520)
    rows_tok, rows_gate, pair_row, be, n_used = _route_plan(idx, wts, n_tok)
    x_pad = jnp.concatenate([xb, jnp.zeros((1, D_MODEL), BF16)], 0)
    y_rows = _experts(x_pad[rows_tok], rows_gate, be, n_used, we_gate, we_up, we_down, layer)
    x = _combine_ln(y_rows, pair_row, shared, x, ln2_g, ln2_b, layer, tm=104)
    return _ple(x, p, ple_gate, ple_proj, layer, tm=1040, tn=512)


def kernel(x_prompt, x_sample, state_ret, state_ssm, state_ssm_conv, state_gdn, state_gdn_conv, p_prompt, p_sample, ev_w_in, ev_w_out, ssm_conv_w, ssm_conv_b, ssm_dt_bias, ssm_a_log, ssm_d, ssm_norm_w, od_w_in, od_w_out, gdn_conv_w, gdn_dt_bias, gdn_a_log, gdn_norm_w, ln1_g, ln1_b, ln2_g, ln2_b, router_w, router_b, exp_w_gate, exp_w_up, exp_w_down, sh_w_gate, sh_w_up, sh_w_down, ple_proj, ple_gate):
    bp, lp = x_prompt.shape[0], x_prompt.shape[1]
    ls = x_sample.shape[1]
    pos_p = jnp.arange(lp, dtype=jnp.int32)
    pos_s = PAST_LEN + jnp.arange(ls, dtype=jnp.int32)
    x = jnp.concatenate([x_prompt.reshape(N_PROMPT, D_MODEL), x_sample.reshape(DEC_BATCH, D_MODEL)], 0)
    p_all = jnp.concatenate([p_prompt.reshape(DEPTH, N_PROMPT, D_PLE), p_sample.reshape(DEPTH, DEC_BATCH, D_PLE)], 1)
    ret_p, ret_s, ssm_p, ssm_s, sconv_p, sconv_s = [], [], [], [], [], []
    gdn_p, gdn_s, gconv_p, gconv_s = [], [], [], []
    half = RET_DK // 2
    ang = pos_p.astype(F32)[:, None] * (ROPE_BASE ** (-jnp.arange(half, dtype=F32) / half))
    cos_p, sin_p = jnp.cos(ang), jnp.sin(ang)
    log_gamma = jnp.log1p(-jnp.exp2(-5.0 - jnp.arange(RET_HEADS, dtype=F32)))
    for i in range(DEPTH):
        j = i // 2
        if i % 2 == 0:
            proj = _dense(x, ev_w_in, (j,), n_cols=EVEN_MAIN, tm=1040, tn=512)
            dt_raw = _dense(x, ev_w_in[j, :, EVEN_MAIN:], tm=1040, tn=SSM_HEADS)
            w = (ssm_conv_w[j], ssm_conv_b[j], ssm_dt_bias[j], ssm_a_log[j], ssm_d[j], ssm_norm_w[j])
            mix, r1 = _retention_prompt(jnp.zeros((N_TOK, EVEN_MIX), F32), proj, cos_p, sin_p, log_gamma)
            dt_t = dt_raw[:N_PROMPT].T.reshape(SSM_HEADS, N_PROMPT // CH, 1, CH)
            mix, r2, cx, cb, cc = _ssd_prompt(mix, proj, dt_t, ssm_conv_w[j], ssm_conv_b[j].reshape(1, -1),
                                              ssm_a_log[j], ssm_dt_bias[j], ssm_d[j], ssm_norm_w[j].reshape(1, -1))
            r2 = r2.reshape(bp, SSM_HEADS, SSM_P, SSM_N)
            r3 = jnp.concatenate([cx, cb, cc], -1)
            ms, q1, q2, q3 = _even_core(proj[N_PROMPT:].reshape(DEC_BATCH, ls, -1),
                                        dt_raw[N_PROMPT:].reshape(DEC_BATCH, ls, -1),
                                        pos_s, state_ret[j], state_ssm[j], state_ssm_conv[j], *w)
            ret_p.append(r1); ssm_p.append(r2); sconv_p.append(r3)
            ret_s.append(q1); ssm_s.append(q2); sconv_s.append(q3)
            mix = lax.dynamic_update_slice(mix, ms.reshape(DEC_BATCH, -1), (N_PROMPT, 0))
            mix_out = _dense(mix, ev_w_out, (j,), tm=520, tn=512)
        else:
            proj = _dense(x, od_w_in, (j,), n_cols=ODD_MAIN, tm=1040, tn=512)
            ab = _dense(x, od_w_in[j, :, ODD_MAIN:], tm=1040, tn=2 * GDN_HV)
            w = (gdn_conv_w[j], gdn_dt_bias[j], gdn_a_log[j], gdn_norm_w[j])
            ab_t = ab[:N_PROMPT].reshape(N_PROMPT // CH, CH, 2, GDN_HK, 2).transpose(2, 3, 0, 4, 1)
            ab_t = ab_t.reshape(2, GDN_HK, N_PROMPT // CH, 1, 2 * CH)
            mix, r1, cq, ck, cv = _gdn_prompt(jnp.zeros((N_TOK, GDN_VW), F32), proj, ab_t[0], ab_t[1], gdn_conv_w[j],
                                              gdn_a_log[j], gdn_dt_bias[j], gdn_norm_w[j].reshape(1, -1))
            r2 = jnp.concatenate([cq, ck, cv], -1)
            ms, q1, q2 = _odd_core(proj[N_PROMPT:].reshape(DEC_BATCH, ls, -1), ab[N_PROMPT:].reshape(DEC_BATCH, ls, -1),
                                   state_gdn[j], state_gdn_conv[j], *w)
            gdn_p.append(r1); gconv_p.append(r2)
            gdn_s.append(q1); gconv_s.append(q2)
            mix = lax.dynamic_update_slice(mix, ms.reshape(DEC_BATCH, -1), (N_PROMPT, 0))
            mix_out = _dense(mix, od_w_out, (j,), tm=520, tn=512)
        x = _post_block(x, mix_out, p_all, i, ln1_g, ln1_b, ln2_g, ln2_b, router_w, router_b,
                        exp_w_gate, exp_w_up, exp_w_down, sh_w_gate, sh_w_up, sh_w_down, ple_proj, ple_gate)
    xp = x[:N_PROMPT].reshape(bp, lp, D_MODEL)
    xs = x[N_PROMPT:].reshape(DEC_BATCH, ls, D_MODEL)
    return (xp, xs, jnp.stack(ret_p), jnp.stack(ret_s), jnp.stack(ssm_p), jnp.stack(ssm_s),
            jnp.stack(sconv_p), jnp.stack(sconv_s), jnp.stack(gdn_p), jnp.stack(gdn_s),
            jnp.stack(gconv_p), jnp.stack(gconv_s))
```

```python
import functools
import math

import jax
import jax.numpy as jnp
from jax import lax
from jax.experimental import pallas as pl
from jax.experimental.pallas import tpu as pltpu

F32 = jnp.float32
BF16 = jnp.bfloat16

D_MODEL = 2048
BATCH = 4
SEQ = 2048
DEPTH = 4
DEC_BATCH = 128
PAST_LEN = 16384
CHUNK = 64
CONV_W = 4
NORM_EPS = 1e-5
RET_HEADS = 8
RET_DK = D_MODEL // RET_HEADS
RET_DV = D_MODEL // RET_HEADS
ROPE_BASE = 10000.0
SSM_HEADS = 32
SSM_P = D_MODEL // SSM_HEADS
SSM_INNER = SSM_HEADS * SSM_P
SSM_GROUPS = 4
SSM_N = 128
SSM_CONV_CH = SSM_INNER + 2 * SSM_GROUPS * SSM_N
GDN_HK = 16
GDN_HV = 32
GDN_DK = 128
GDN_DV = 128
GDN_QK = GDN_HK * GDN_DK
GDN_VW = GDN_HV * GDN_DV
GDN_CONV_CH = 2 * GDN_QK + GDN_VW
EVEN_SPLITS = (RET_HEADS * RET_DK, RET_HEADS * RET_DK, RET_HEADS * RET_DV, RET_HEADS * RET_DV,
               SSM_INNER, SSM_CONV_CH, SSM_HEADS)
EVEN_MAIN = sum(EVEN_SPLITS[:-1])
EVEN_MIX = RET_HEADS * RET_DV + SSM_INNER
ODD_SPLITS = (GDN_CONV_CH, GDN_VW, GDN_HV, GDN_HV)
ODD_MAIN = GDN_CONV_CH + GDN_VW
N_EXPERTS = 64
TOP_K = 8
D_EXPERT = D_MODEL // 4
ROUTED_SCALE = 2.5
D_PLE = 256
ALPHA = (2 * DEPTH) ** 0.25

N_PROMPT = BATCH * SEQ
N_TOK = N_PROMPT + DEC_BATCH
EXPERT_ROWS = 256
VMEM_LIMIT = 56 * 1024 * 1024


def _split(t, sizes):
    out, start = [], 0
    for s in sizes:
        out.append(t[..., start:start + s])
        start += s
    return out


def _dense_body(x_ref, w_ref, o_ref, wb_ref):
    @pl.when(pl.program_id(1) == 0)
    def _():
        wb_ref[...] = w_ref[...].astype(BF16)

    o_ref[...] = jnp.dot(x_ref[...].astype(BF16), wb_ref[...], preferred_element_type=F32)


def _dense(x, w, lead=(), *, n_cols=None, tm, tn):
    m, k = x.shape
    n_cols = w.shape[-1] if n_cols is None else n_cols
    assert m % tm == 0 and n_cols % tn == 0
    nl = len(lead)
    w_block = (None,) * nl + (k, tn)
    return pl.pallas_call(
        _dense_body,
        grid=(n_cols // tn, m // tm),
        in_specs=[pl.BlockSpec((tm, k), lambda j, i: (i, 0)),
                  pl.BlockSpec(w_block, lambda j, i: lead + (0, j))],
        out_specs=pl.BlockSpec((tm, tn), lambda j, i: (i, j)),
        out_shape=jax.ShapeDtypeStruct((m, n_cols), F32),
        scratch_shapes=[pltpu.VMEM((k, tn), BF16)],
        compiler_params=pltpu.CompilerParams(dimension_semantics=("arbitrary", "arbitrary"),
                                             vmem_limit_bytes=VMEM_LIMIT),
    )(x, w)


def _dense_f32_body(x_ref, w_ref, o_ref):
    o_ref[...] = jnp.dot(x_ref[...].astype(BF16), w_ref[...].astype(BF16), preferred_element_type=F32)


def _dense_f32(x, w, lead=(), *, tm):
    m, k = x.shape
    n = w.shape[-1]
    nl = len(lead)
    return pl.pallas_call(
        _dense_f32_body,
        grid=(m // tm,),
        in_specs=[pl.BlockSpec((tm, k), lambda i: (i, 0)),
                  pl.BlockSpec((None,) * nl + (k, n), lambda i: lead + (0, 0))],
        out_specs=pl.BlockSpec((tm, n), lambda i: (i, 0)),
        out_shape=jax.ShapeDtypeStruct((m, n), F32),
        compiler_params=pltpu.CompilerParams(dimension_semantics=("arbitrary",),
                                             vmem_limit_bytes=VMEM_LIMIT),
    )(x, w)


def _ln_router_body(x_ref, m_ref, g_ref, b_ref, rw_ref, rb_ref, x1_ref, xb_ref, idx_ref, wts_ref):
    acc = ALPHA * x_ref[...] + m_ref[...]
    xc = acc - jnp.mean(acc, -1, keepdims=True)
    var = jnp.mean(xc * xc, -1, keepdims=True)
    x1 = xc * lax.rsqrt(var + NORM_EPS) * g_ref[...] + b_ref[...]
    x1_ref[...] = x1
    xb = x1.astype(BF16)
    xb_ref[...] = xb
    scores = jax.nn.sigmoid(jnp.dot(xb, rw_ref[...].astype(BF16), preferred_element_type=F32))
    sel = scores + rb_ref[...]
    tm, ne = sel.shape
    lane = _iota2((tm, ne), 1).astype(F32)
    slot = _iota2((tm, TOP_K), 1)
    idx = jnp.zeros((tm, TOP_K), F32)
    wts = jnp.zeros((tm, TOP_K), F32)
    for r in range(TOP_K):
        best = jnp.max(sel, axis=-1, keepdims=True)
        arg = jnp.min(jnp.where(sel == best, lane, float(ne)), axis=-1, keepdims=True)
        pick = lane == arg
        idx = jnp.where(slot == r, arg, idx)
        wts = jnp.where(slot == r, jnp.sum(jnp.where(pick, scores, 0.0), axis=-1, keepdims=True), wts)
        sel = jnp.where(pick, -jnp.inf, sel)
    idx_ref[...] = idx.astype(jnp.int32)
    wts_ref[...] = wts / jnp.sum(wts, -1, keepdims=True) * ROUTED_SCALE


def _ln_router(x, mix_out, ln_g, ln_b, router_w, router_b, layer, *, tm):
    n_tok, d = x.shape
    row = lambda w: pl.BlockSpec((tm, w), lambda i: (i, 0))
    par = lambda *shape: pl.BlockSpec((None,) + shape, lambda i: (layer,) + (0,) * len(shape))
    return pl.pallas_call(
        _ln_router_body,
        grid=(n_tok // tm,),
        in_specs=[row(d), row(d), par(1, d), par(1, d), par(d, N_EXPERTS), par(1, N_EXPERTS)],
        out_specs=[row(d), row(d), row(TOP_K), row(TOP_K)],
        out_shape=[jax.ShapeDtypeStruct((n_tok, d), F32), jax.ShapeDtypeStruct((n_tok, d), BF16),
                   jax.ShapeDtypeStruct((n_tok, TOP_K), jnp.int32), jax.ShapeDtypeStruct((n_tok, TOP_K), F32)],
        compiler_params=pltpu.CompilerParams(dimension_semantics=("arbitrary",), vmem_limit_bytes=VMEM_LIMIT),
    )(x, mix_out, ln_g.reshape(DEPTH, 1, d), ln_b.reshape(DEPTH, 1, d), router_w, router_b.reshape(DEPTH, 1, N_EXPERTS))


def _shared_body(x_ref, wg_ref, wu_ref, wd_ref, o_ref, wgb, wub, wdb):
    @pl.when(pl.program_id(0) == 0)
    def _():
        wgb[...] = wg_ref[...].astype(BF16)
        wub[...] = wu_ref[...].astype(BF16)
        wdb[...] = wd_ref[...].astype(BF16)

    xb = x_ref[...]
    g = jnp.dot(xb, wgb[...], preferred_element_type=F32)
    u = jnp.dot(xb, wub[...], preferred_element_type=F32)
    h = (g * jax.nn.sigmoid(g)) * u
    o_ref[...] = jnp.dot(h.astype(BF16), wdb[...], preferred_element_type=F32)


def _shared_expert(xb, ws_gate, ws_up, ws_down, layer, *, tm):
    n_tok, d = xb.shape
    ds = ws_gate.shape[-1]
    return pl.pallas_call(
        _shared_body,
        grid=(n_tok // tm,),
        in_specs=[pl.BlockSpec((tm, d), lambda i: (i, 0)),
                  pl.BlockSpec((None, d, ds), lambda i: (layer, 0, 0)),
                  pl.BlockSpec((None, d, ds), lambda i: (layer, 0, 0)),
                  pl.BlockSpec((None, ds, d), lambda i: (layer, 0, 0))],
        out_specs=pl.BlockSpec((tm, d), lambda i: (i, 0)),
        out_shape=jax.ShapeDtypeStruct((n_tok, d), F32),
        scratch_shapes=[pltpu.VMEM((d, ds), BF16), pltpu.VMEM((d, ds), BF16), pltpu.VMEM((ds, d), BF16)],
        compiler_params=pltpu.CompilerParams(dimension_semantics=("arbitrary",), vmem_limit_bytes=VMEM_LIMIT),
    )(xb, ws_gate, ws_up, ws_down)


def _ple_body(x_ref, xc_ref, p_ref, wg_ref, wp_ref, o_ref, wgb, wpb):
    @pl.when(pl.program_id(1) == 0)
    def _():
        wgb[...] = wg_ref[...].astype(BF16)
        wpb[...] = wp_ref[...].astype(BF16)

    gate = jnp.dot(x_ref[...].astype(BF16), wgb[...], preferred_element_type=F32)
    emb = jnp.dot(p_ref[...].astype(BF16), wpb[...], preferred_element_type=F32)
    o_ref[...] = xc_ref[...] + jax.nn.sigmoid(gate) * emb


def _ple(x, p, ple_gate, ple_proj, layer, *, tm, tn):
    n_tok, d = x.shape
    dp = p.shape[-1]
    return pl.pallas_call(
        _ple_body,
        grid=(d // tn, n_tok // tm),
        in_specs=[pl.BlockSpec((tm, d), lambda j, i: (i, 0)),
                  pl.BlockSpec((tm, tn), lambda j, i: (i, j)),
                  pl.BlockSpec((None, tm, dp), lambda j, i: (layer, i, 0)),
                  pl.BlockSpec((None, d, tn), lambda j, i: (layer, 0, j)),
                  pl.BlockSpec((None, dp, tn), lambda j, i: (layer, 0, j))],
        out_specs=pl.BlockSpec((tm, tn), lambda j, i: (i, j)),
        out_shape=jax.ShapeDtypeStruct((n_tok, d), F32),
        scratch_shapes=[pltpu.VMEM((d, tn), BF16), pltpu.VMEM((dp, tn), BF16)],
        compiler_params=pltpu.CompilerParams(dimension_semantics=("arbitrary", "arbitrary"),
                                             vmem_limit_bytes=VMEM_LIMIT),
    )(x, x, p, ple_gate, ple_proj)


def _experts_body(be_ref, nu_ref, cur_ref, nxt_ref, x_hbm, gate_ref, wg_ref, wu_ref, wd_ref, y_ref,
                  wgb, wub, wdb, xbuf, sem):
    b = pl.program_id(0)
    n_used = nu_ref[0]
    slot = b % 2
    e = be_ref[b]
    prev = be_ref[jnp.maximum(b - 1, 0)]
    tm = xbuf.shape[1]

    def row_copy(src_row, dst_slot, r):
        return pltpu.make_async_copy(x_hbm.at[pl.ds(src_row, 1), :], xbuf.at[dst_slot, pl.ds(r, 1), :],
                                     sem.at[dst_slot])

    def issue(idx_ref, dst_slot):
        def body(r, carry):
            row_copy(idx_ref[0, r], dst_slot, r).start()
            return carry
        lax.fori_loop(0, tm, body, 0, unroll=8)

    @pl.when((b == 0) & (n_used > 0))
    def _():
        issue(cur_ref, 0)

    @pl.when(b + 1 < n_used)
    def _():
        issue(nxt_ref, 1 - slot)

    @pl.when((b == 0) | (e != prev))
    def _():
        wgb[...] = wg_ref[...].astype(BF16)
        wub[...] = wu_ref[...].astype(BF16)
        wdb[...] = wd_ref[...].astype(BF16)

    @pl.when(b < n_used)
    def _():
        def wait_body(r, carry):
            row_copy(0, slot, r).wait()
            return carry
        lax.fori_loop(0, tm, wait_body, 0, unroll=8)
        xb = xbuf[slot].astype(BF16)
        g = jnp.dot(xb, wgb[...], preferred_element_type=F32)
        u = jnp.dot(xb, wub[...], preferred_element_type=F32)
        h = (g * jax.nn.sigmoid(g)) * u
        y_ref[...] = jnp.dot(h.astype(BF16), wdb[...], preferred_element_type=F32) * gate_ref[...]

    @pl.when(b >= nu_ref[0])
    def _():
        y_ref[...] = jnp.zeros_like(y_ref)


def _experts(x, rows_tok, row_gate, block_expert, n_used, we_gate, we_up, we_down, layer):
    d = x.shape[1]
    rows = rows_tok.shape[0]
    tm = EXPERT_ROWS
    n_blocks = rows // tm
    idx = rows_tok.reshape(n_blocks, 1, tm)
    idx_spec = lambda f: pl.BlockSpec((None, 1, tm), f, memory_space=pltpu.SMEM)
    grid_spec = pltpu.PrefetchScalarGridSpec(
        num_scalar_prefetch=2,
        grid=(n_blocks,),
        in_specs=[idx_spec(lambda b, be, nu: (b, 0, 0)),
                  idx_spec(lambda b, be, nu: (jnp.minimum(b + 1, n_blocks - 1), 0, 0)),
                  pl.BlockSpec(memory_space=pl.ANY),
                  pl.BlockSpec((tm, 1), lambda b, be, nu: (b, 0)),
                  pl.BlockSpec((None, None, d, D_EXPERT), lambda b, be, nu: (layer, be[b], 0, 0)),
                  pl.BlockSpec((None, None, d, D_EXPERT), lambda b, be, nu: (layer, be[b], 0, 0)),
                  pl.BlockSpec((None, None, D_EXPERT, d), lambda b, be, nu: (layer, be[b], 0, 0))],
        out_specs=pl.BlockSpec((tm, d), lambda b, be, nu: (b, 0)),
        scratch_shapes=[pltpu.VMEM((d, D_EXPERT), BF16), pltpu.VMEM((d, D_EXPERT), BF16),
                        pltpu.VMEM((D_EXPERT, d), BF16), pltpu.VMEM((2, tm, d), F32),
                        pltpu.SemaphoreType.DMA((2,))],
    )
    return pl.pallas_call(
        _experts_body,
        grid_spec=grid_spec,
        out_shape=jax.ShapeDtypeStruct((rows, d), F32),
        compiler_params=pltpu.CompilerParams(dimension_semantics=("arbitrary",),
                                             vmem_limit_bytes=VMEM_LIMIT),
    )(block_expert, n_used, idx, idx, x, row_gate, we_gate, we_up, we_down)


def _route_plan(idx, wts, n_tok):
    tm = EXPERT_ROWS
    n_pairs = n_tok * TOP_K
    n_blocks = (n_pairs + N_EXPERTS * (tm - 1) + tm - 1) // tm
    flat_e = idx.reshape(-1).astype(jnp.int32)
    pos = jnp.arange(n_pairs, dtype=jnp.int32)
    w_bits = lax.bitcast_convert_type(wts.reshape(-1), jnp.int32)
    sorted_e, order, sorted_w = lax.sort((flat_e, pos, w_bits), num_keys=1, is_stable=True)
    is_first = jnp.concatenate([jnp.ones((1,), bool), sorted_e[1:] != sorted_e[:-1]])
    start = lax.cummax(jnp.where(is_first, pos, 0), axis=0)
    prev_start = jnp.concatenate([jnp.zeros((1,), jnp.int32), start[:-1]])
    inc = jnp.where(is_first & (pos > 0), (pos - prev_start + tm - 1) // tm * tm, 0)
    dest = jnp.cumsum(inc) + pos - start
    packed = jnp.stack([order // TOP_K, sorted_w], axis=1)
    rows = jnp.zeros((n_blocks * tm, 2), jnp.int32).at[dest].set(packed, unique_indices=True)
    rows_tok = rows[:, 0]
    rows_gate = lax.bitcast_convert_type(rows[:, 1], F32)
    _, pair_row = lax.sort((order, dest), num_keys=1)
    grp_end = jnp.searchsorted(sorted_e, jnp.arange(N_EXPERTS, dtype=jnp.int32), side='right').astype(jnp.int32)
    counts = grp_end - jnp.concatenate([jnp.zeros((1,), jnp.int32), grp_end[:-1]])
    pad_end = jnp.cumsum((counts + tm - 1) // tm * tm)
    n_used = (pad_end[-1] // tm).astype(jnp.int32)
    blk = jnp.arange(n_blocks, dtype=jnp.int32)
    be = jnp.minimum(jnp.searchsorted(pad_end, blk * tm, side='right'), N_EXPERTS - 1).astype(jnp.int32)
    be = jnp.where(blk < n_used, be, be[jnp.maximum(n_used - 1, 0)])
    return rows_tok, rows_gate.reshape(-1, 1), pair_row.reshape(n_tok, TOP_K), be, n_used.reshape(1)


def _combine_body(cur_ref, nxt_ref, y_hbm, sh_ref, x_ref, g_ref, b_ref, o_ref, buf, sem, *, tm):
    i = pl.program_id(0)
    n = pl.num_programs(0)
    slot = i % 2
    n_rows = TOP_K * tm

    def row_copy(src_row, dst_slot, r):
        return pltpu.make_async_copy(y_hbm.at[pl.ds(src_row, 1), :], buf.at[dst_slot, pl.ds(r, 1), :],
                                     sem.at[dst_slot])

    def issue(idx_ref, dst_slot):
        def body(r, carry):
            row_copy(idx_ref[0, r], dst_slot, r).start()
            return carry
        lax.fori_loop(0, n_rows, body, 0, unroll=8)

    @pl.when(i == 0)
    def _():
        issue(cur_ref, 0)

    @pl.when(i + 1 < n)
    def _():
        issue(nxt_ref, 1 - slot)

    def wait_body(r, carry):
        row_copy(0, slot, r).wait()
        return carry
    lax.fori_loop(0, n_rows, wait_body, 0, unroll=8)

    acc = ALPHA * x_ref[...] + sh_ref[...]
    for kk in range(TOP_K):
        acc = acc + buf[slot, kk * tm:(kk + 1) * tm, :]
    xc = acc - jnp.mean(acc, -1, keepdims=True)
    var = jnp.mean(xc * xc, -1, keepdims=True)
    o_ref[...] = xc * lax.rsqrt(var + NORM_EPS) * g_ref[...] + b_ref[...]


def _combine_ln(y_rows, pair_row, shared, x, ln_g, ln_b, layer, *, tm):
    n_tok, d = x.shape
    n_steps = n_tok // tm
    idx = pair_row.reshape(n_steps, tm, TOP_K).transpose(0, 2, 1).reshape(n_steps, 1, TOP_K * tm)
    idx_spec = lambda f: pl.BlockSpec((None, 1, TOP_K * tm), f, memory_space=pltpu.SMEM)
    return pl.pallas_call(
        functools.partial(_combine_body, tm=tm),
        grid=(n_steps,),
        in_specs=[idx_spec(lambda i: (i, 0, 0)),
                  idx_spec(lambda i: (jnp.minimum(i + 1, n_steps - 1), 0, 0)),
                  pl.BlockSpec(memory_space=pl.ANY),
                  pl.BlockSpec((tm, d), lambda i: (i, 0)),
                  pl.BlockSpec((tm, d), lambda i: (i, 0)),
                  pl.BlockSpec((None, 1, d), lambda i: (layer, 0, 0)),
                  pl.BlockSpec((None, 1, d), lambda i: (layer, 0, 0))],
        out_specs=pl.BlockSpec((tm, d), lambda i: (i, 0)),
        out_shape=jax.ShapeDtypeStruct((n_tok, d), F32),
        scratch_shapes=[pltpu.VMEM((2, TOP_K * tm, d), F32), pltpu.SemaphoreType.DMA((2,))],
        compiler_params=pltpu.CompilerParams(dimension_semantics=("arbitrary",), vmem_limit_bytes=VMEM_LIMIT),
    )(idx, idx, y_rows, shared, x, ln_g.reshape(DEPTH, 1, d), ln_b.reshape(DEPTH, 1, d))


CH = 64


def _dot(a, b):
    return jnp.dot(a.astype(BF16), b.astype(BF16), preferred_element_type=F32)


def _dot_nt(a, b):
    return lax.dot_general(a.astype(BF16), b.astype(BF16), (((1,), (1,)), ((), ())), preferred_element_type=F32)


def _dot_tn(a, b):
    return lax.dot_general(a.astype(BF16), b.astype(BF16), (((0,), (0,)), ((), ())), preferred_element_type=F32)


def _split2(a):
    hi = a.astype(BF16)
    lo = (a - hi.astype(F32)).astype(BF16)
    return hi, lo


def _dot_sel(a, sel):
    hi = a.astype(BF16)
    r = a - hi.astype(F32)
    mid = r.astype(BF16)
    lo = (r - mid.astype(F32)).astype(BF16)
    d = lambda x: jnp.dot(x, sel, preferred_element_type=F32)
    return d(hi) + d(mid) + d(lo)


def _dot_hi(a, b):
    ah, al = _split2(a)
    bh, bl = _split2(b)
    d = lambda x, y: jnp.dot(x, y, preferred_element_type=F32)
    return d(ah, bh) + d(ah, bl) + d(al, bh)


def _iota2(shape, dim):
    return lax.broadcasted_iota(jnp.int32, shape, dim)


def _decay_terms(g_row, c):
    ii = _iota2((c, c), 0)
    mm = _iota2((c, c), 1)
    l_incl = jnp.where(ii >= mm, g_row, 0.0)
    l_excl = jnp.where(mm > ii, g_row, 0.0)
    m2 = _iota2((c, 256), 0)
    j2 = _iota2((c, 256), 1)
    rhs1 = jnp.where(j2 >= 128, 1.0, jnp.where(m2 > j2, 1.0, 0.0)).astype(BF16)
    ones = jnp.ones((c, 128), BF16)
    out1 = _dot_sel(l_incl, rhs1)
    return out1[:, :c], out1[:, 128:], _dot_sel(l_excl, ones)


def _col_bcast(row, c):
    ii = _iota2((c, c), 0)
    mm = _iota2((c, c), 1)
    return _dot_sel(jnp.where(ii == mm, row, 0.0), jnp.ones((c, 128), BF16))


def _softplus(x):
    return jnp.maximum(x, 0.0) + jnp.log1p(jnp.exp(-jnp.abs(x)))


def _silu(x):
    return x * jax.nn.sigmoid(x)


def _conv_block(x_ref, e_ref, w_ref, rows):
    e_ref[8:rows + 8, :] = x_ref[...]
    w = w_ref[...]
    return (e_ref[8:rows + 8, :] * w[3:4, :] + e_ref[7:rows + 7, :] * w[2:3, :]
            + e_ref[6:rows + 6, :] * w[1:2, :] + e_ref[5:rows + 5, :] * w[0:1, :])


def _conv_carry(e_ref, rows):
    e_ref[0:8, :] = e_ref[rows:rows + 8, :]


def _ret_body(lg_ref, cd_ref, mix_ref, q_ref, k_ref, v_ref, g_ref, cos_ref, sin_ref, o_ref, s_ref, *, rows, dk):
    del mix_ref
    h = pl.program_id(1)
    lg = lg_ref[h]

    @pl.when(pl.program_id(2) == 0)
    def _():
        s_ref[...] = jnp.zeros_like(s_ref)

    cos = cos_ref[...]
    sin = sin_ref[...]
    half = dk // 2

    def rot(x):
        x1 = x[:, :half]
        x2 = x[:, half:]
        return jnp.concatenate([x1 * cos - x2 * sin, x1 * sin + x2 * cos], axis=-1)

    q = rot(q_ref[...])
    k = rot(k_ref[...]) * (dk ** -0.5)
    v = v_ref[...]
    c = CH
    ii = _iota2((c, c), 0)
    jj = _iota2((c, c), 1)
    dmask = jnp.where(ii >= jj, jnp.exp(lg * jnp.maximum(ii - jj, 0).astype(F32)), 0.0)
    ti = _iota2((c, 1), 0).astype(F32)
    q_dec = jnp.exp((ti + 1.0) * lg)
    k_dec = jnp.exp((c - 1.0 - ti) * lg)
    s = s_ref[...]
    for ch in range(rows // c):
        sl = slice(ch * c, (ch + 1) * c)
        qc, kc, vc = q[sl], k[sl], v[sl]
        att = _dot_nt(qc, kc) * dmask
        o = _dot(att, vc) + _dot(qc * q_dec, s)
        s = s * cd_ref[h] + _dot_tn(kc * k_dec, vc)
        oc = o - jnp.mean(o, -1, keepdims=True)
        on = oc * lax.rsqrt(jnp.mean(oc * oc, -1, keepdims=True) + NORM_EPS)
        o_ref[sl, :] = _silu(g_ref[sl, :]) * on
    s_ref[...] = s


def _retention_prompt(mix, proj, cos, sin, log_gamma, *, rows=256):
    ntb = SEQ // rows
    heads, dk = RET_HEADS, RET_DK
    cdec = jnp.exp(CH * log_gamma)
    smem = pl.BlockSpec(memory_space=pltpu.SMEM)

    def blk(part):
        return pl.BlockSpec((rows, dk), lambda b, h, t: (b * ntb + t, part * heads + h))

    return pl.pallas_call(
        functools.partial(_ret_body, rows=rows, dk=dk),
        grid=(BATCH, heads, ntb),
        in_specs=[smem, smem, pl.BlockSpec(memory_space=pl.ANY), blk(0), blk(1), blk(2), blk(3),
                  pl.BlockSpec((rows, dk // 2), lambda b, h, t: (t, 0)),
                  pl.BlockSpec((rows, dk // 2), lambda b, h, t: (t, 0))],
        out_specs=[pl.BlockSpec((rows, dk), lambda b, h, t: (b * ntb + t, h)),
                   pl.BlockSpec((None, None, dk, dk), lambda b, h, t: (b, h, 0, 0))],
        out_shape=[jax.ShapeDtypeStruct(mix.shape, F32),
                   jax.ShapeDtypeStruct((BATCH, heads, dk, dk), F32)],
        input_output_aliases={2: 0},
        compiler_params=pltpu.CompilerParams(dimension_semantics=("arbitrary",) * 3, vmem_limit_bytes=VMEM_LIMIT),
    )(log_gamma, cdec, mix, proj, proj, proj, proj, cos, sin)


def _ssd_body(alog_ref, dtb_ref, dsk_ref, mix_ref, x_ref, bm_ref, cm_ref, z_ref, dt_ref,
              wx_ref, wb_ref, wc_ref, bx_ref, bb_ref, bc_ref, nw_ref,
              o_ref, st_ref, cx_ref, cb_ref, cc_ref, ex, eb, ec, s_t, *, rows, hpg, hd):
    del mix_ref
    g = pl.program_id(1)
    tb = pl.program_id(2)
    last_tb = pl.num_programs(2) - 1
    c = CH
    nch = rows // c

    @pl.when(tb == 0)
    def _():
        s_t[...] = jnp.zeros_like(s_t)
        ex[0:8, :] = jnp.zeros((8, ex.shape[1]), F32)
        eb[0:8, :] = jnp.zeros((8, eb.shape[1]), F32)
        ec[0:8, :] = jnp.zeros((8, ec.shape[1]), F32)

    xs = _silu(_conv_block(x_ref, ex, wx_ref, rows) + bx_ref[...])
    bm = _silu(_conv_block(bm_ref, eb, wb_ref, rows) + bb_ref[...])
    cm = _silu(_conv_block(cm_ref, ec, wc_ref, rows) + bc_ref[...])

    @pl.when(tb == last_tb)
    def _():
        cx_ref[...] = ex[rows + 5:rows + 8, :]
        cb_ref[...] = eb[rows + 5:rows + 8, :]
        cc_ref[...] = ec[rows + 5:rows + 8, :]

    _conv_carry(ex, rows)
    _conv_carry(eb, rows)
    _conv_carry(ec, rows)

    ii = _iota2((c, c), 0)
    jj = _iota2((c, c), 1)
    incl = ii >= jj
    nw = nw_ref[...]
    for ch in range(nch):
        sl = slice(ch * c, (ch + 1) * c)
        xc = xs[sl]
        bc = bm[sl]
        cc = cm[sl]
        cbm = _dot_nt(cc, bc)
        st = s_t[...]
        ystate = _dot(cc, st)
        ys, wxs, decs = [], [], []
        for r in range(hpg):
            h = g * hpg + r
            dt_row = _softplus(dt_ref[r, ch] + dtb_ref[h])
            a_vec = -jnp.exp(jnp.zeros((1, c), F32) + alog_ref[h])
            dmat, cs_cb, rest_cb = _decay_terms(dt_row * a_vec, c)
            dt_cb = _col_bcast(dt_row, c)
            lm = jnp.where(incl, jnp.exp(jnp.where(incl, dmat, 0.0)), 0.0)
            x_r = xc[:, r * hd:(r + 1) * hd]
            y_r = _dot(cbm * lm * dt_row, x_r) + ystate[:, r * hd:(r + 1) * hd] * jnp.exp(cs_cb[:, :hd])
            ys.append(y_r + dsk_ref[h] * x_r)
            wxs.append(x_r * (jnp.exp(rest_cb[:, :hd]) * dt_cb[:, :hd]))
            decs.append(jnp.exp(cs_cb[c - 1:c, :hd]))
        s_t[...] = st * jnp.concatenate(decs, axis=-1) + _dot_tn(bc, jnp.concatenate(wxs, axis=-1))
        y = jnp.concatenate(ys, axis=-1) * _silu(z_ref[sl, :])
        o_ref[sl, :] = y * lax.rsqrt(jnp.mean(y * y, -1, keepdims=True) + NORM_EPS) * nw

    @pl.when(tb == last_tb)
    def _():
        st_ref[...] = s_t[...].T


def _ssd_prompt(mix, proj, dt_t, conv_w, conv_b, a_log, dt_bias, d_skip, norm_w, *, rows=128):
    ntb = SEQ // rows
    nch = rows // CH
    groups, hpg, hd, n_state = SSM_GROUPS, SSM_HEADS // SSM_GROUPS, SSM_P, SSM_N
    gw = hpg * hd
    z_col = (4 * RET_HEADS * RET_DK) // gw
    x_col = z_col + SSM_INNER // gw
    b_col = (x_col * gw + SSM_INNER) // n_state
    c_col = b_col + groups
    wb_col = SSM_INNER // n_state
    wc_col = wb_col + groups
    out_col = (RET_HEADS * RET_DV) // gw
    smem = pl.BlockSpec(memory_space=pltpu.SMEM)
    in_specs = [
        smem, smem, smem,
        pl.BlockSpec(memory_space=pl.ANY),
        pl.BlockSpec((rows, gw), lambda b, g, t: (b * ntb + t, x_col + g)),
        pl.BlockSpec((rows, n_state), lambda b, g, t: (b * ntb + t, b_col + g)),
        pl.BlockSpec((rows, n_state), lambda b, g, t: (b * ntb + t, c_col + g)),
        pl.BlockSpec((rows, gw), lambda b, g, t: (b * ntb + t, z_col + g)),
        pl.BlockSpec((hpg, nch, 1, CH), lambda b, g, t: (g, b * ntb + t, 0, 0)),
        pl.BlockSpec((CONV_W, gw), lambda b, g, t: (0, g)),
        pl.BlockSpec((CONV_W, n_state), lambda b, g, t: (0, wb_col + g)),
        pl.BlockSpec((CONV_W, n_state), lambda b, g, t: (0, wc_col + g)),
        pl.BlockSpec((1, gw), lambda b, g, t: (0, g)),
        pl.BlockSpec((1, n_state), lambda b, g, t: (0, wb_col + g)),
        pl.BlockSpec((1, n_state), lambda b, g, t: (0, wc_col + g)),
        pl.BlockSpec((1, gw), lambda b, g, t: (0, g)),
    ]
    out_specs = [
        pl.BlockSpec((rows, gw), lambda b, g, t: (b * ntb + t, out_col + g)),
        pl.BlockSpec((None, gw, n_state), lambda b, g, t: (b, g, 0)),
        pl.BlockSpec((None, CONV_W - 1, gw), lambda b, g, t: (b, 0, g)),
        pl.BlockSpec((None, CONV_W - 1, n_state), lambda b, g, t: (b, 0, g)),
        pl.BlockSpec((None, CONV_W - 1, n_state), lambda b, g, t: (b, 0, g)),
    ]
    out_shape = [
        jax.ShapeDtypeStruct(mix.shape, F32),
        jax.ShapeDtypeStruct((BATCH, SSM_INNER, n_state), F32),
        jax.ShapeDtypeStruct((BATCH, CONV_W - 1, SSM_INNER), F32),
        jax.ShapeDtypeStruct((BATCH, CONV_W - 1, groups * n_state), F32),
        jax.ShapeDtypeStruct((BATCH, CONV_W - 1, groups * n_state), F32),
    ]
    scratch = [pltpu.VMEM((rows + 8, gw), F32), pltpu.VMEM((rows + 8, n_state), F32),
               pltpu.VMEM((rows + 8, n_state), F32), pltpu.VMEM((n_state, gw), F32)]
    return pl.pallas_call(
        functools.partial(_ssd_body, rows=rows, hpg=hpg, hd=hd),
        grid=(BATCH, groups, ntb),
        in_specs=in_specs, out_specs=out_specs, out_shape=out_shape, scratch_shapes=scratch,
        input_output_aliases={3: 0},
        compiler_params=pltpu.CompilerParams(dimension_semantics=("arbitrary",) * 3, vmem_limit_bytes=VMEM_LIMIT),
    )(a_log, dt_bias, d_skip, mix, proj, proj, proj, proj, dt_t, conv_w, conv_w, conv_w,
      conv_b, conv_b, conv_b, norm_w)


def _inv_unit_lower(a, c):
    ii = _iota2((c, c), 0)
    jj = _iota2((c, c), 1)
    n = -a
    p = jnp.where(ii == jj, 1.0, 0.0) + n
    for _ in range(c.bit_length() - 2):
        n = _dot_hi(n, n)
        p = p + _dot_hi(p, n)
    return p


def _gdn_body(alog_ref, dtb_ref, mix_ref, q_ref, k_ref, v_ref, z_ref, a_ref, b_ref, wq_ref, wk_ref, wv_ref, nw_ref,
              o_ref, s_ref, cq_ref, ck_ref, cv_ref, eq, ek, ev, *, rows, rep, dk, dv):
    del mix_ref
    hk = pl.program_id(1)
    tb = pl.program_id(2)
    last_tb = pl.num_programs(2) - 1
    c = CH
    nch = rows // c

    @pl.when(tb == 0)
    def _():
        s_ref[...] = jnp.zeros_like(s_ref)
        eq[0:8, :] = jnp.zeros((8, eq.shape[1]), F32)
        ek[0:8, :] = jnp.zeros((8, ek.shape[1]), F32)
        ev[0:8, :] = jnp.zeros((8, ev.shape[1]), F32)

    q = _silu(_conv_block(q_ref, eq, wq_ref, rows))
    k = _silu(_conv_block(k_ref, ek, wk_ref, rows))
    v = _silu(_conv_block(v_ref, ev, wv_ref, rows))

    @pl.when(tb == last_tb)
    def _():
        cq_ref[...] = eq[rows + 5:rows + 8, :]
        ck_ref[...] = ek[rows + 5:rows + 8, :]
        cv_ref[...] = ev[rows + 5:rows + 8, :]

    _conv_carry(eq, rows)
    _conv_carry(ek, rows)
    _conv_carry(ev, rows)

    q = q * lax.rsqrt(jnp.sum(q * q, -1, keepdims=True) + 1e-6) * (dk ** -0.5)
    k = k * lax.rsqrt(jnp.sum(k * k, -1, keepdims=True) + 1e-6)

    c2 = 2 * c
    ii = _iota2((c2, c2), 0)
    jj = _iota2((c2, c2), 1)
    same = (ii >= c) == (jj >= c)
    ti = ii & (c - 1)
    tj = jj & (c - 1)
    incl = same & (ti >= tj)
    strict = same & (ti > tj)
    later = same & (tj > ti)
    diag = ii == jj
    top = ii < c
    col2 = _iota2((c2, 2 * c2), 1)
    row2 = _iota2((c2, 2 * c2), 0)
    sel1 = jnp.where(col2 >= c2, 1.0, jnp.where(((row2 >= c) == (col2 >= c)) & ((row2 & (c - 1)) > (col2 & (c - 1))),
                                                 1.0, 0.0)).astype(BF16)
    sel2 = jnp.where((_iota2((2 * c2, 2 * c2), 0) >= c2) == (_iota2((2 * c2, 2 * c2), 1) >= c2), 1.0, 0.0).astype(BF16)
    eye = jnp.where(diag, 1.0, 0.0)
    lane = _iota2((1, c2), 1)
    alog = jnp.where(lane < c, alog_ref[hk * 2], alog_ref[hk * 2 + 1])
    dtb = jnp.where(lane < c, dtb_ref[hk * 2], dtb_ref[hk * 2 + 1])
    neg_a = -jnp.exp(alog)
    nw = nw_ref[...]

    pre = []
    for ch in range(nch):
        sl = slice(ch * c, (ch + 1) * c)
        q2 = jnp.concatenate([q[sl], q[sl]], axis=0)
        k2 = jnp.concatenate([k[sl], k[sl]], axis=0)
        v2 = jnp.concatenate([v[sl, :dv], v[sl, dv:]], axis=0)
        g_row = neg_a * _softplus(a_ref[ch] + dtb)
        beta_row = jax.nn.sigmoid(b_ref[ch])
        out1 = _dot_sel(jnp.where(incl, g_row, 0.0), sel1)
        out2 = _dot_sel(jnp.concatenate([jnp.where(later, g_row, 0.0), jnp.where(diag, beta_row, 0.0)], axis=1), sel2)
        gcum_cb, rest_cb, beta_cb = out1[:, c2:], out2[:, :c2], out2[:, c2:]
        decay = jnp.where(incl, jnp.exp(jnp.where(incl, out1[:, :c2], 0.0)), 0.0)
        eg = jnp.exp(gcum_cb)
        kb2 = k2 * beta_cb
        pre.append(dict(
            n=-jnp.where(strict, _dot_nt(kb2, k2) * decay, 0.0),
            rhs=jnp.concatenate([v2 * beta_cb, kb2 * eg], axis=1),
            qe=q2 * eg,
            qk=_dot_nt(q2, k2) * decay,
            kw=k2 * jnp.exp(rest_cb),
            declast=jnp.concatenate([eg[c - 1:c, :], eg[c2 - 1:c2, :]], axis=1)))

    ps = [eye + d["n"] for d in pre]
    ms = [_dot_hi(d["n"], d["n"]) for d in pre]
    n_steps = c.bit_length() - 2
    for step in range(n_steps):
        if step < n_steps - 1:
            rs = [_dot_hi(jnp.concatenate([p, m], axis=0), m) for p, m in zip(ps, ms)]
            ps = [p + r[:c2] for p, r in zip(ps, rs)]
            ms = [r[c2:] for r in rs]
        else:
            ps = [p + _dot_hi(p, m) for p, m in zip(ps, ms)]
    sols = [_dot_hi(p, d["rhs"]) for p, d in zip(ps, pre)]

    s_cat = jnp.concatenate([s_ref[0], s_ref[1]], axis=1)
    for ch in range(nch):
        sl = slice(ch * c, (ch + 1) * c)
        d = pre[ch]
        sol = sols[ch]
        r = _dot(jnp.concatenate([sol[:, dv:], d["qe"]], axis=0), s_cat)
        u = sol[:, :dv] - jnp.where(top, r[:c2, :dv], r[:c2, dv:])
        o = jnp.where(top, r[c2:, :dv], r[c2:, dv:]) + _dot(d["qk"], u)
        u_bd = jnp.concatenate([jnp.where(top, u, 0.0), jnp.where(top, 0.0, u)], axis=1)
        s_cat = s_cat * d["declast"] + _dot_tn(d["kw"], u_bd)
        zz = jnp.concatenate([z_ref[sl, :dv], z_ref[sl, dv:]], axis=0)
        res = o * lax.rsqrt(jnp.mean(o * o, -1, keepdims=True) + NORM_EPS) * nw * _silu(zz)
        o_ref[sl, :dv] = res[:c]
        o_ref[sl, dv:] = res[c:]
    s_ref[0] = s_cat[:, :dv]
    s_ref[1] = s_cat[:, dv:]


def _gdn_prompt(mix, proj, a_t, b_t, conv_w, a_log, dt_bias, norm_w, *, rows=256):
    hk, hv, dk, dv = GDN_HK, GDN_HV, GDN_DK, GDN_DV
    rep = hv // hk
    assert rep == 2 and dk == dv
    ntb = SEQ // rows
    nch = rows // CH
    vw = rep * dv
    kcol = hk
    vcol = (2 * GDN_QK) // vw
    zcol = (2 * GDN_QK + GDN_VW) // vw
    smem = pl.BlockSpec(memory_space=pltpu.SMEM)
    in_specs = [
        smem, smem,
        pl.BlockSpec(memory_space=pl.ANY),
        pl.BlockSpec((rows, dk), lambda b, h, t: (b * ntb + t, h)),
        pl.BlockSpec((rows, dk), lambda b, h, t: (b * ntb + t, kcol + h)),
        pl.BlockSpec((rows, vw), lambda b, h, t: (b * ntb + t, vcol + h)),
        pl.BlockSpec((rows, vw), lambda b, h, t: (b * ntb + t, zcol + h)),
        pl.BlockSpec((None, nch, 1, rep * CH), lambda b, h, t: (h, b * ntb + t, 0, 0)),
        pl.BlockSpec((None, nch, 1, rep * CH), lambda b, h, t: (h, b * ntb + t, 0, 0)),
        pl.BlockSpec((CONV_W, dk), lambda b, h, t: (0, h)),
        pl.BlockSpec((CONV_W, dk), lambda b, h, t: (0, kcol + h)),
        pl.BlockSpec((CONV_W, vw), lambda b, h, t: (0, vcol + h)),
        pl.BlockSpec((1, dv), lambda b, h, t: (0, 0)),
    ]
    out_specs = [
        pl.BlockSpec((rows, vw), lambda b, h, t: (b * ntb + t, h)),
        pl.BlockSpec((None, rep, dk, dv), lambda b, h, t: (b, h, 0, 0)),
        pl.BlockSpec((None, CONV_W - 1, dk), lambda b, h, t: (b, 0, h)),
        pl.BlockSpec((None, CONV_W - 1, dk), lambda b, h, t: (b, 0, h)),
        pl.BlockSpec((None, CONV_W - 1, vw), lambda b, h, t: (b, 0, h)),
    ]
    out_shape = [
        jax.ShapeDtypeStruct(mix.shape, F32),
        jax.ShapeDtypeStruct((BATCH, hv, dk, dv), F32),
        jax.ShapeDtypeStruct((BATCH, CONV_W - 1, GDN_QK), F32),
        jax.ShapeDtypeStruct((BATCH, CONV_W - 1, GDN_QK), F32),
        jax.ShapeDtypeStruct((BATCH, CONV_W - 1, GDN_VW), F32),
    ]
    scratch = [pltpu.VMEM((rows + 8, dk), F32), pltpu.VMEM((rows + 8, dk), F32), pltpu.VMEM((rows + 8, vw), F32)]
    return pl.pallas_call(
        functools.partial(_gdn_body, rows=rows, rep=rep, dk=dk, dv=dv),
        grid=(BATCH, hk, ntb),
        in_specs=in_specs, out_specs=out_specs, out_shape=out_shape, scratch_shapes=scratch,
        input_output_aliases={2: 0},
        compiler_params=pltpu.CompilerParams(dimension_semantics=("arbitrary",) * 3, vmem_limit_bytes=VMEM_LIMIT),
    )(a_log, dt_bias, mix, proj, proj, proj, proj, a_t, b_t, conv_w, conv_w, conv_w, norm_w)


def _layer_norm(xf, g, b):
    xc = xf - jnp.mean(xf, -1, keepdims=True)
    var = jnp.mean(xc * xc, -1, keepdims=True)
    return xc * lax.rsqrt(var + NORM_EPS) * g + b


def _rms(xf):
    return xf * lax.rsqrt(jnp.mean(xf * xf, -1, keepdims=True) + NORM_EPS)


def _head_ln(xf):
    xc = xf - jnp.mean(xf, -1, keepdims=True)
    return xc * lax.rsqrt(jnp.mean(xc * xc, -1, keepdims=True) + NORM_EPS)


def _l2norm(xf):
    return xf * lax.rsqrt(jnp.sum(xf * xf, -1, keepdims=True) + 1e-6)


def _rotary(t, pos):
    half = t.shape[-1] // 2
    inv = ROPE_BASE ** (-jnp.arange(half, dtype=F32) / half)
    ang = pos.astype(F32)[:, None] * inv
    cos = jnp.cos(ang)[None, :, None, :]
    sin = jnp.sin(ang)[None, :, None, :]
    t1, t2 = t[..., :half], t[..., half:]
    return jnp.concatenate([t1 * cos - t2 * sin, t1 * sin + t2 * cos], -1)


def _chunk_len(L):
    return CHUNK if L % CHUNK == 0 else L


def _to_chunks(t, c):
    return t.reshape(t.shape[0], t.shape[1] // c, c, *t.shape[2:]).swapaxes(0, 1)


def _from_chunks(t):
    t = t.swapaxes(0, 1)
    return t.reshape(t.shape[0], t.shape[1] * t.shape[2], *t.shape[3:])


def _causal_conv(x, prev, w, b):
    L = x.shape[1]
    xp = jnp.concatenate([prev.astype(x.dtype), x], axis=1)
    y = xp[:, CONV_W - 1:] * w[CONV_W - 1]
    for i in range(CONV_W - 1):
        y = y + xp[:, i:i + L] * w[i]
    if b is not None:
        y = y + b
    return y, xp[:, L:]


def _retention_scan(q, k, v, s0, log_gamma):
    L = q.shape[1]
    c = _chunk_len(L)
    idx = jnp.arange(c, dtype=F32)
    diff = idx[:, None] - idx[None, :]
    dmask = jnp.exp(jnp.where((diff >= 0)[None], log_gamma[:, None, None] * diff[None], -jnp.inf))
    q_dec = jnp.exp((idx[:, None] + 1.0) * log_gamma)[:, :, None]
    k_dec = jnp.exp((c - 1.0 - idx)[:, None] * log_gamma)[:, :, None]
    c_dec = jnp.exp(c * log_gamma)[:, None, None]

    def step(s, inp):
        qc, kc, vc = inp
        att = jnp.einsum('bihd,bjhd->bhij', qc, kc) * dmask
        o = jnp.einsum('bhij,bjhe->bihe', att, vc) + jnp.einsum('bihd,bhde->bihe', qc * q_dec, s)
        s = s * c_dec + jnp.einsum('bjhd,bjhe->bhde', kc * k_dec, vc)
        return s, o

    s, o = lax.scan(step, s0, (_to_chunks(q, c), _to_chunks(k, c), _to_chunks(v, c)))
    return _from_chunks(o), s


def _ssd_scan(x, dt, a, bm, cm, s0):
    L = x.shape[1]
    c = _chunk_len(L)
    ar = jnp.arange(c)
    mask5 = (ar[:, None] >= ar[None, :])[None, :, :, None, None]

    def step(s, inp):
        xc, dtc, bc, cc = inp
        cs = jnp.cumsum(dtc * a, axis=1)
        lmat = jnp.exp(jnp.where(mask5, cs[:, :, None] - cs[:, None, :], -jnp.inf))
        cb = jnp.einsum('bign,bjgn->bijg', cc, bc)
        y = jnp.einsum('bijg,bijgr,bjgr,bjgrp->bigrp', cb, lmat, dtc, xc)
        y = y + jnp.einsum('bign,bgrpn->bigrp', cc, s) * jnp.exp(cs)[..., None]
        last = cs[:, -1]
        w_end = jnp.exp(last[:, None] - cs) * dtc
        s = s * jnp.exp(last)[..., None, None] + jnp.einsum('bjgr,bjgn,bjgrp->bgrpn', w_end, bc, xc)
        return s, y

    s, y = lax.scan(step, s0, (_to_chunks(x, c), _to_chunks(dt, c), _to_chunks(bm, c), _to_chunks(cm, c)))
    return _from_chunks(y), s


def _gdn_scan(q, k, v, g, beta, s0):
    L = q.shape[1]
    dv = v.shape[-1]
    c = _chunk_len(L)
    ar = jnp.arange(c)
    incl = ar[:, None] >= ar[None, :]
    strict = ar[:, None] > ar[None, :]
    eye = jnp.eye(c, dtype=F32)

    def step(s, inp):
        qc, kc, vc, gc, bc = inp
        gcum = jnp.cumsum(gc, axis=1)
        gh = gcum.swapaxes(1, 2)
        decay = jnp.exp(jnp.where(incl, gh[..., :, None] - gh[..., None, :], -jnp.inf))
        kb = kc * bc[..., None]
        a_mat = jnp.where(strict, jnp.einsum('bihd,bjhd->bhij', kb, kc) * decay, 0.0)
        rhs = jnp.concatenate([(vc * bc[..., None]).swapaxes(1, 2),
                               (kb * jnp.exp(gcum)[..., None]).swapaxes(1, 2)], -1)
        sol = lax.linalg.triangular_solve(a_mat + eye, rhs, left_side=True, lower=True, unit_diagonal=True)
        u = sol[..., :dv] - jnp.einsum('bhid,bhde->bhie', sol[..., dv:], s)
        qk = jnp.einsum('bihd,bjhd->bhij', qc, kc) * decay
        o = jnp.einsum('bihd,bhde->bihe', qc * jnp.exp(gcum)[..., None], s) + jnp.einsum('bhij,bhje->bihe', qk, u)
        g_last = gcum[:, -1]
        s = s * jnp.exp(g_last)[..., None, None] + jnp.einsum(
            'bjhd,bhje->bhde', kc * jnp.exp(g_last[:, None] - gcum)[..., None], u)
        return s, o

    xs = (_to_chunks(q, c), _to_chunks(k, c), _to_chunks(v, c), _to_chunks(g, c), _to_chunks(beta, c))
    s, o = lax.scan(step, s0, xs)
    return _from_chunks(o), s


def _even_core(proj, dt_raw, pos, s_ret, s_ssm, conv_prev, conv_w, conv_b, dt_bias, a_log, d_skip, norm_w):
    bsz, L, _ = proj.shape
    q, k, v, g, z, xbc = _split(proj, EVEN_SPLITS[:-1])
    q = _rotary(q.reshape(bsz, L, RET_HEADS, RET_DK), pos)
    k = _rotary(k.reshape(bsz, L, RET_HEADS, RET_DK), pos) * RET_DK ** -0.5
    v = v.reshape(bsz, L, RET_HEADS, RET_DV)
    log_gamma = jnp.log1p(-jnp.exp2(-5.0 - jnp.arange(RET_HEADS, dtype=F32)))
    o_ret, s_ret_new = _retention_scan(q, k, v, s_ret, log_gamma)
    o_ret = jax.nn.silu(g) * _head_ln(o_ret).reshape(bsz, L, -1)
    xbc, conv_new = _causal_conv(xbc, conv_prev, conv_w, conv_b)
    xs, bm, cm = _split(jax.nn.silu(xbc), (SSM_INNER, SSM_GROUPS * SSM_N, SSM_GROUPS * SSM_N))
    r = SSM_HEADS // SSM_GROUPS
    xs = xs.reshape(bsz, L, SSM_GROUPS, r, SSM_P)
    dt = jax.nn.softplus(dt_raw + dt_bias).reshape(bsz, L, SSM_GROUPS, r)
    a = -jnp.exp(a_log).reshape(SSM_GROUPS, r)
    y, s_ssm_new = _ssd_scan(xs, dt, a, bm.reshape(bsz, L, SSM_GROUPS, SSM_N), cm.reshape(bsz, L, SSM_GROUPS, SSM_N),
                             s_ssm.reshape(bsz, SSM_GROUPS, r, SSM_P, SSM_N))
    y = y + d_skip.reshape(SSM_GROUPS, r, 1) * xs
    y = y.reshape(bsz, L, SSM_INNER) * jax.nn.silu(z)
    y = _rms(y.reshape(bsz, L, SSM_GROUPS, -1)).reshape(bsz, L, SSM_INNER) * norm_w
    mix = jnp.concatenate([o_ret, y], -1)
    return mix, s_ret_new, s_ssm_new.reshape(bsz, SSM_HEADS, SSM_P, SSM_N), conv_new


def _odd_core(proj, ab, s_gdn, conv_prev, conv_w, dt_bias, a_log, norm_w):
    bsz, L, _ = proj.shape
    qkv, z = _split(proj, ODD_SPLITS[:2])
    a, b = _split(ab, ODD_SPLITS[2:])
    qkv, conv_new = _causal_conv(qkv, conv_prev, conv_w, None)
    q, k, v = _split(jax.nn.silu(qkv), (GDN_QK, GDN_QK, GDN_VW))
    rep = GDN_HV // GDN_HK
    q = jnp.repeat(_l2norm(q.reshape(bsz, L, GDN_HK, GDN_DK)), rep, axis=2) * GDN_DK ** -0.5
    k = jnp.repeat(_l2norm(k.reshape(bsz, L, GDN_HK, GDN_DK)), rep, axis=2)
    v = v.reshape(bsz, L, GDN_HV, GDN_DV)
    g = -jnp.exp(a_log) * jax.nn.softplus(a + dt_bias)
    beta = jax.nn.sigmoid(b)
    o, s_new = _gdn_scan(q, k, v, g, beta, s_gdn)
    o = _rms(o) * norm_w * jax.nn.silu(z.reshape(bsz, L, GDN_HV, GDN_DV))
    return o.reshape(bsz, L, GDN_VW), s_new, conv_new


def _post_block(x, mix_out, p, layer, ln1_g, ln1_b, ln2_g, ln2_b, router_w, router_b, we_gate, we_up, we_down,
                ws_gate, ws_up, ws_down, ple_proj, ple_gate):
    n_tok = x.shape[0]
    x, xb, idx, wts = _ln_router(x, mix_out, ln1_g, ln1_b, router_w, router_b, layer, tm=520)
    shared = _shared_expert(xb, ws_gate, ws_up, ws_down, layer, tm=520)
    rows_tok, rows_gate, pair_row, be, n_used = _route_plan(idx, wts, n_tok)
    y_rows = _experts(x, rows_tok, rows_gate, be, n_used, we_gate, we_up, we_down, layer)
    x = _combine_ln(y_rows, pair_row, shared, x, ln2_g, ln2_b, layer, tm=104)
    return _ple(x, p, ple_gate, ple_proj, layer, tm=1040, tn=512)


def kernel(x_prompt, x_sample, state_ret, state_ssm, state_ssm_conv, state_gdn, state_gdn_conv, p_prompt, p_sample, ev_w_in, ev_w_out, ssm_conv_w, ssm_conv_b, ssm_dt_bias, ssm_a_log, ssm_d, ssm_norm_w, od_w_in, od_w_out, gdn_conv_w, gdn_dt_bias, gdn_a_log, gdn_norm_w, ln1_g, ln1_b, ln2_g, ln2_b, router_w, router_b, exp_w_gate, exp_w_up, exp_w_down, sh_w_gate, sh_w_up, sh_w_down, ple_proj, ple_gate):
    bp, lp = x_prompt.shape[0], x_prompt.shape[1]
    ls = x_sample.shape[1]
    pos_p = jnp.arange(lp, dtype=jnp.int32)
    pos_s = PAST_LEN + jnp.arange(ls, dtype=jnp.int32)
    x = jnp.concatenate([x_prompt.reshape(N_PROMPT, D_MODEL), x_sample.reshape(DEC_BATCH, D_MODEL)], 0)
    p_all = jnp.concatenate([p_prompt.reshape(DEPTH, N_PROMPT, D_PLE), p_sample.reshape(DEPTH, DEC_BATCH, D_PLE)], 1)
    ret_p, ret_s, ssm_p, ssm_s, sconv_p, sconv_s = [], [], [], [], [], []
    gdn_p, gdn_s, gconv_p, gconv_s = [], [], [], []
    half = RET_DK // 2
    ang = pos_p.astype(F32)[:, None] * (ROPE_BASE ** (-jnp.arange(half, dtype=F32) / half))
    cos_p, sin_p = jnp.cos(ang), jnp.sin(ang)
    log_gamma = jnp.log1p(-jnp.exp2(-5.0 - jnp.arange(RET_HEADS, dtype=F32)))
    for i in range(DEPTH):
        j = i // 2
        if i % 2 == 0:
            proj = _dense(x, ev_w_in, (j,), n_cols=EVEN_MAIN, tm=1040, tn=512)
            dt_raw = _dense(x, ev_w_in[j, :, EVEN_MAIN:], tm=1040, tn=SSM_HEADS)
            w = (ssm_conv_w[j], ssm_conv_b[j], ssm_dt_bias[j], ssm_a_log[j], ssm_d[j], ssm_norm_w[j])
            mix, r1 = _retention_prompt(jnp.zeros((N_TOK, EVEN_MIX), F32), proj, cos_p, sin_p, log_gamma)
            dt_t = dt_raw[:N_PROMPT].T.reshape(SSM_HEADS, N_PROMPT // CH, 1, CH)
            mix, r2, cx, cb, cc = _ssd_prompt(mix, proj, dt_t, ssm_conv_w[j], ssm_conv_b[j].reshape(1, -1),
                                              ssm_a_log[j], ssm_dt_bias[j], ssm_d[j], ssm_norm_w[j].reshape(1, -1))
            r2 = r2.reshape(bp, SSM_HEADS, SSM_P, SSM_N)
            r3 = jnp.concatenate([cx, cb, cc], -1)
            ms, q1, q2, q3 = _even_core(proj[N_PROMPT:].reshape(DEC_BATCH, ls, -1),
                                        dt_raw[N_PROMPT:].reshape(DEC_BATCH, ls, -1),
                                        pos_s, state_ret[j], state_ssm[j], state_ssm_conv[j], *w)
            ret_p.append(r1); ssm_p.append(r2); sconv_p.append(r3)
            ret_s.append(q1); ssm_s.append(q2); sconv_s.append(q3)
            mix = lax.dynamic_update_slice(mix, ms.reshape(DEC_BATCH, -1), (N_PROMPT, 0))
            mix_out = _dense(mix, ev_w_out, (j,), tm=520, tn=512)
        else:
            proj = _dense(x, od_w_in, (j,), n_cols=ODD_MAIN, tm=1040, tn=512)
            ab = _dense(x, od_w_in[j, :, ODD_MAIN:], tm=1040, tn=2 * GDN_HV)
            w = (gdn_conv_w[j], gdn_dt_bias[j], gdn_a_log[j], gdn_norm_w[j])
            ab_t = ab[:N_PROMPT].reshape(N_PROMPT // CH, CH, 2, GDN_HK, 2).transpose(2, 3, 0, 4, 1)
            ab_t = ab_t.reshape(2, GDN_HK, N_PROMPT // CH, 1, 2 * CH)
            mix, r1, cq, ck, cv = _gdn_prompt(jnp.zeros((N_TOK, GDN_VW), F32), proj, ab_t[0], ab_t[1], gdn_conv_w[j],
                                              gdn_a_log[j], gdn_dt_bias[j], gdn_norm_w[j].reshape(1, -1))
            r2 = jnp.concatenate([cq, ck, cv], -1)
            ms, q1, q2 = _odd_core(proj[N_PROMPT:].reshape(DEC_BATCH, ls, -1), ab[N_PROMPT:].reshape(DEC_BATCH, ls, -1),
                                   state_gdn[j], state_gdn_conv[j], *w)
            gdn_p.append(r1); gconv_p.append(r2)
            gdn_s.append(q1); gconv_s.append(q2)
            mix = lax.dynamic_update_slice(mix, ms.reshape(DEC_BATCH, -1), (N_PROMPT, 0))
            mix_out = _dense(mix, od_w_out, (j,), tm=520, tn=512)
        x = _post_block(x, mix_out, p_all, i, ln1_g, ln1_b, ln2_g, ln2_b, router_w, router_b,
                        exp_w_gate, exp_w_up, exp_w_down, sh_w_gate, sh_w_up, sh_w_down, ple_proj, ple_gate)
    xp = x[:N_PROMPT].reshape(bp, lp, D_MODEL)
    xs = x[N_PROMPT:].reshape(DEC_BATCH, ls, D_MODEL)
    return (xp, xs, jnp.stack(ret_p), jnp.stack(ret_s), jnp.stack(ssm_p), jnp.stack(ssm_s),
            jnp.stack(sconv_p), jnp.stack(sconv_s), jnp.stack(gdn_p), jnp.stack(gdn_s),
            jnp.stack(gconv_p), jnp.stack(gconv_s))
```
